```python
import math
import jax, jax.numpy as jnp
from jax import lax
import numpy as np

D_MODEL = 2048
BATCH = 2
SEQ = 8192
DEPTH = 2

N_MIXERS = 2
EPS = 1e-6
HEAD_DIM_A = 128
N_HEADS_A = 16
DILATED_GROUPS = ((128, 1), (512, 4), (2048, 16))
N_GROUPS_A = len(DILATED_GROUPS)
WIDTH_A = N_HEADS_A * HEAD_DIM_A
N_HEADS_B = 8
QK_DIM_B = D_MODEL // N_HEADS_B
V_DIM_B = 2 * QK_DIM_B
CHUNK_B = 128
ROPE_BASE = 10000.0
D_FF = 5632
CONV_WIDTH = 3
N_LAYERS_A = (DEPTH + 1) // 2
N_LAYERS_B = DEPTH // 2

kernel_name = "hybrid_dilated_retention_convffn_adaln"


def rmsnorm(x, g):
    xf = x.astype(jnp.float32)
    y = xf * lax.rsqrt(jnp.mean(xf * xf, axis=-1, keepdims=True) + EPS) * g.astype(jnp.float32)
    return y.astype(x.dtype)


def modulate(h, shift, scale):
    return h * (1 + scale[:, None, :]) + shift[:, None, :]


def dilated_group_attention(q, k, v, dilation, steps):
    B, S, H, E = q.shape
    span = dilation * steps
    Lp = -(-S // span) * span
    n = Lp // dilation
    nb = n // steps

    def to_blocks(t):
        t = jnp.pad(t, ((0, 0), (0, Lp - S), (0, 0), (0, 0)))
        t = t.reshape(B, n, dilation, H, E).transpose(0, 3, 2, 1, 4)
        return t.reshape(B, H, dilation, nb, steps, E)

    qb, kb, vb = to_blocks(q), to_blocks(k), to_blocks(v)

    def with_prev(t):
        prev = jnp.concatenate([jnp.zeros_like(t[:, :, :, :1]), t[:, :, :, :-1]], axis=3)
        return jnp.concatenate([prev, t], axis=4)

    kk, vv = with_prev(kb), with_prev(vb)
    s = jnp.einsum('bhrnqe,bhrnke->bhrnqk', qb, kk).astype(jnp.float32) * (E ** -0.5)
    qi = jnp.arange(steps)[:, None]
    kj = jnp.arange(2 * steps)[None, :]
    dist = qi + steps - kj
    band = (dist >= 0) & (dist <= steps)
    blk = jnp.arange(nb)[:, None, None]
    mask = band[None] & ((blk > 0) | (kj >= steps)[None])
    s = jnp.where(mask, s, -jnp.inf)
    m = jnp.max(s, axis=-1, keepdims=True)
    p = jnp.exp(s - m)
    den = jnp.sum(p, axis=-1)
    o = jnp.einsum('bhrnqk,bhrnke->bhrnqe', p, vv.astype(jnp.float32)) / den[..., None]
    lse = m[..., 0] + jnp.log(den)
    o = o.reshape(B, H, dilation, n, E).transpose(0, 3, 2, 1, 4).reshape(B, Lp, H, E)[:, :S]
    lse = lse.reshape(B, H, dilation, n).transpose(0, 3, 2, 1).reshape(B, Lp, H)[:, :S]
    return o, lse


def dilated_attention_mixer(h, w_in, w_out):
    B, S, _ = h.shape
    proj = (h @ w_in).reshape(B, S, N_GROUPS_A, 3, N_HEADS_A, HEAD_DIM_A)
    outs, lses = [], []
    for g, (window, dil) in enumerate(DILATED_GROUPS):
        o, l = dilated_group_attention(proj[:, :, g, 0], proj[:, :, g, 1], proj[:, :, g, 2], dil, window // dil)
        outs.append(o)
        lses.append(l)
    alpha = jax.nn.softmax(jnp.stack(lses, axis=0), axis=0)
    o = jnp.sum(alpha[..., None] * jnp.stack(outs, axis=0), axis=0)
    return o.reshape(B, S, WIDTH_A).astype(h.dtype) @ w_out


def rotary(x, pos):
    half = x.shape[-1] // 2
    freqs = ROPE_BASE ** (-jnp.arange(half, dtype=jnp.float32) / half)
    ang = pos[:, None] * freqs[None, :]
    cos, sin = jnp.cos(ang)[:, None, :], jnp.sin(ang)[:, None, :]
    x1, x2 = x[..., :half], x[..., half:]
    return jnp.concatenate([x1 * cos - x2 * sin, x1 * sin + x2 * cos], axis=-1)


def retention_mixer(h, w_in, gn_g, w_out):
    B, S, _ = h.shape
    H, dk, dv, C = N_HEADS_B, QK_DIM_B, V_DIM_B, CHUNK_B
    proj = h @ w_in
    q, k, v, g = jnp.split(proj, [H * dk, 2 * H * dk, 2 * H * dk + H * dv], axis=-1)
    pos = jnp.arange(S, dtype=jnp.float32)
    q = rotary(q.reshape(B, S, H, dk).astype(jnp.float32), pos)
    k = rotary(k.reshape(B, S, H, dk).astype(jnp.float32), pos) * (dk ** -0.5)
    v = v.reshape(B, S, H, dv).astype(jnp.float32)
    nc = S // C

    def chunks(t):
        return t.reshape(B, nc, C, H, t.shape[-1]).transpose(0, 3, 1, 2, 4)

    qc, kc, vc = chunks(q), chunks(k), chunks(v)
    log_gamma = jnp.log1p(-jnp.exp2(-5.0 - jnp.arange(H, dtype=jnp.float32)))
    idx = jnp.arange(C, dtype=jnp.float32)
    rel = idx[:, None] - idx[None, :]
    decay = jnp.where(rel >= 0, jnp.exp(log_gamma[:, None, None] * jnp.maximum(rel, 0.0)), 0.0)
    inner = jnp.einsum('bhnqd,bhnkd->bhnqk', qc, kc) * decay[None, :, None]
    inner = jnp.einsum('bhnqk,bhnke->bhnqe', inner, vc)
    xi = jnp.exp(log_gamma[:, None] * (idx + 1.0))
    zeta = jnp.exp(log_gamma[:, None] * (C - 1.0 - idx))
    gamma_c = jnp.exp(log_gamma * C)

    def step(R, xs):
        qi, ki, vi = xs
        cross = jnp.einsum('bhqd,bhde->bhqe', qi, R) * xi[None, :, :, None]
        R = gamma_c[None, :, None, None] * R + jnp.einsum('bhkd,bhke->bhde', ki, vi * zeta[None, :, :, None])
        return R, cross

    R0 = jnp.zeros((B, H, dk, dv), jnp.float32)
    _, cross = lax.scan(step, R0, (qc.transpose(2, 0, 1, 3, 4), kc.transpose(2, 0, 1, 3, 4), vc.transpose(2, 0, 1, 3, 4)))
    y = inner + cross.transpose(1, 2, 0, 3, 4)
    y = y.transpose(0, 2, 3, 1, 4).reshape(B, S, H, dv)
    mu = jnp.mean(y, axis=-1, keepdims=True)
    var = jnp.mean(jnp.square(y - mu), axis=-1, keepdims=True)
    y = ((y - mu) * lax.rsqrt(var + EPS)).reshape(B, S, H * dv) * gn_g.astype(jnp.float32)
    out = jax.nn.silu(g.astype(jnp.float32)) * y
    return out.astype(h.dtype) @ w_out


def conv_gated_mlp(h, w_up, conv_w, conv_b, w_down):
    u = h @ w_up
    u = lax.conv_general_dilated(u, conv_w.astype(u.dtype), window_strides=(1,), padding=[(CONV_WIDTH - 1, 0)],
                                 dimension_numbers=('NWC', 'WIO', 'NWC'), feature_group_count=2 * D_FF) + conv_b
    a, b = jnp.split(u, 2, axis=-1)
    return (jax.nn.silu(a) * b) @ w_down


def setup_inputs(seed: int = 0) -> dict:
    key = jax.random.key(seed)
    ks = jax.random.split(key, 20)
    D, F = D_MODEL, D_FF
    nrm = lambda k, shape, fan_in: jax.random.normal(k, shape, jnp.float32) * (fan_in ** -0.5)
    return {
        "x": jax.random.normal(ks[0], (BATCH, SEQ, D), jnp.float32),
        "c": jax.random.normal(ks[1], (BATCH, D), jnp.float32),
        "mod_w": nrm(ks[2], (DEPTH, D, 6 * D), D) * 0.5,
        "mod_b": jax.random.normal(ks[3], (DEPTH, 6 * D), jnp.float32) * 0.01,
        "norm_mix": 1.0 + 0.02 * jax.random.normal(ks[4], (DEPTH, D), jnp.float32),
        "w_in_a": nrm(ks[5], (N_LAYERS_A, D, N_GROUPS_A * 3 * WIDTH_A), D),
        "w_out_a": nrm(ks[6], (N_LAYERS_A, WIDTH_A, D), WIDTH_A),
        "w_in_b": nrm(ks[7], (N_LAYERS_B, D, 2 * N_HEADS_B * QK_DIM_B + 2 * N_HEADS_B * V_DIM_B), D),
        "gn_b": 1.0 + 0.02 * jax.random.normal(ks[8], (N_LAYERS_B, N_HEADS_B * V_DIM_B), jnp.float32),
        "w_out_b": nrm(ks[9], (N_LAYERS_B, N_HEADS_B * V_DIM_B, D), N_HEADS_B * V_DIM_B),
        "norm_ffn": 1.0 + 0.02 * jax.random.normal(ks[10], (DEPTH, D), jnp.float32),
        "ffn_up": nrm(ks[11], (DEPTH, D, 2 * F), D),
        "ffn_conv_w": nrm(ks[12], (DEPTH, CONV_WIDTH, 1, 2 * F), CONV_WIDTH),
        "ffn_conv_b": 0.01 * jax.random.normal(ks[13], (DEPTH, 2 * F), jnp.float32),
        "ffn_down": nrm(ks[14], (DEPTH, F, D), F),
        "final_norm": 1.0 + 0.02 * jax.random.normal(ks[15], (D,), jnp.float32),
    }


def reference(x, c, mod_w, mod_b, norm_mix, w_in_a, w_out_a, w_in_b, gn_b, w_out_b,
              norm_ffn, ffn_up, ffn_conv_w, ffn_conv_b, ffn_down, final_norm):
    c_act = jax.nn.silu(c)
    for i in range(DEPTH):
        mod = c_act @ mod_w[i] + mod_b[i]
        sh_a, sc_a, g_a, sh_m, sc_m, g_m = jnp.split(mod, 6, axis=-1)
        hm = modulate(rmsnorm(x, norm_mix[i]), sh_a, sc_a)
        if i % N_MIXERS == 0:
            y = dilated_attention_mixer(hm, w_in_a[i // N_MIXERS], w_out_a[i // N_MIXERS])
        else:
            j = i // N_MIXERS
            y = retention_mixer(hm, w_in_b[j], gn_b[j], w_out_b[j])
        x = x + g_a[:, None, :] * y
        hf = modulate(rmsnorm(x, norm_ffn[i]), sh_m, sc_m)
        x = x + g_m[:, None, :] * conv_gated_mlp(hf, ffn_up[i], ffn_conv_w[i], ffn_conv_b[i], ffn_down[i])
    return rmsnorm(x, final_norm)
```

```python
import functools
import math

import jax
import jax.numpy as jnp
from jax import lax
from jax.experimental import pallas as pl
from jax.experimental.pallas import tpu as pltpu

F32 = jnp.float32
BF16 = jnp.bfloat16

EPS = 1e-6
HEAD_DIM_A = 128
N_HEADS_A = 16
WIDTH_A = N_HEADS_A * HEAD_DIM_A
DILATED_GROUPS = ((128, 1), (512, 4), (2048, 16))
ATTN_STEPS = 128
N_HEADS_B = 8
CHUNK_B = 128
ROPE_BASE = 10000.0
CONV_WIDTH = 3

VMEM_LIMIT_BYTES = 56 * 1024 * 1024
LANES = 128
HALO_ROWS = 16

ROW_TILE = 512
PROJ_COL_TILE = 2048
OUT_COL_TILE = 1024
FFN_COL_TILE = 512
ATTN_ROW_TILE = 512
ATTN_HEADS_PER_STEP = 4
RET_ROW_TILE = 512
MOD_COL_TILE = 1024


def _params(n_axes):
    return pltpu.CompilerParams(
        dimension_semantics=("arbitrary",) * n_axes,
        vmem_limit_bytes=VMEM_LIMIT_BYTES)


def _rmsnorm(x, gain):
    ms = jnp.mean(x * x, axis=-1, keepdims=True)
    return x * lax.rsqrt(ms + EPS) * gain


def _norm_mod(x, gain, shift, scale):
    return _rmsnorm(x, gain) * (1.0 + scale) + shift


def _mod_kernel(c_ref, w_ref, b_ref, o_ref):
    c = c_ref[...]
    c_act = (c * jax.nn.sigmoid(c)).astype(BF16)
    o_ref[...] = jnp.dot(c_act, w_ref[...].astype(BF16),
                         preferred_element_type=F32) + b_ref[...]


def _modulation(c, mod_w, mod_b):
    depth, d, n = mod_w.shape
    b = c.shape[0]
    tn = MOD_COL_TILE
    return pl.pallas_call(
        _mod_kernel,
        out_shape=jax.ShapeDtypeStruct((depth, b, n), F32),
        grid=(depth, n // tn),
        in_specs=[
            pl.BlockSpec((b, d), lambda l, j: (0, 0)),
            pl.BlockSpec((None, d, tn), lambda l, j: (l, 0, j)),
            pl.BlockSpec((None, 1, tn), lambda l, j: (l, 0, j)),
        ],
        out_specs=pl.BlockSpec((None, b, tn), lambda l, j: (l, 0, j)),
        compiler_params=_params(2),
    )(c, mod_w, mod_b.reshape(depth, 1, n))


def _proj_kernel(x_ref, gain_ref, shift_ref, scale_ref, w_ref, o_ref, h_ref):
    @pl.when(pl.program_id(3) == 0)
    def _():
        h_ref[...] = _norm_mod(x_ref[...], gain_ref[...], shift_ref[...],
                               scale_ref[...]).astype(BF16)

    o_ref[...] = jnp.dot(h_ref[...], w_ref[...],
                         preferred_element_type=F32).astype(o_ref.dtype)


def _project(x, gain, shift, scale, w, col0, n_cols, dilation):
    b, s, d = x.shape
    sd = s // dilation
    tm = min(ROW_TILE, sd)
    tn = PROJ_COL_TILE
    col_block0 = col0 // tn
    xv = x.reshape(b, sd, dilation * d)
    return pl.pallas_call(
        _proj_kernel,
        out_shape=jax.ShapeDtypeStruct((b, dilation, sd, n_cols), BF16),
        grid=(b, dilation, sd // tm, n_cols // tn),
        in_specs=[
            pl.BlockSpec((None, tm, d), lambda bi, r, n, j: (bi, n, r)),
            pl.BlockSpec((1, d), lambda bi, r, n, j: (0, 0)),
            pl.BlockSpec((None, 1, d), lambda bi, r, n, j: (bi, 0, 0)),
            pl.BlockSpec((None, 1, d), lambda bi, r, n, j: (bi, 0, 0)),
            pl.BlockSpec((d, tn), lambda bi, r, n, j: (0, col_block0 + j)),
        ],
        out_specs=pl.BlockSpec((None, None, tm, tn), lambda bi, r, n, j: (bi, r, n, j)),
        scratch_shapes=[pltpu.VMEM((tm, d), BF16)],
        compiler_params=_params(4),
    )(xv, gain, shift, scale, w)


def _attn_kernel(q_ref, k_ref, v_ref, kp_ref, vp_ref, o_ref, lse_ref, *, tq, hb):
    n = pl.program_id(2)
    hblk = pl.program_id(3)
    w = ATTN_STEPS
    nq = tq // w
    scale = HEAD_DIM_A ** -0.5

    @pl.when(hblk == 0)
    def _():
        lse_ref[...] = jnp.zeros_like(lse_ref)

    qi = lax.broadcasted_iota(jnp.int32, (w, 2 * w), 0)
    kj = lax.broadcasted_iota(jnp.int32, (w, 2 * w), 1)
    band = (kj >= qi) & (kj <= qi + w)
    band_first = band & ((kj >= w) | (n > 0))
    lane = lax.broadcasted_iota(jnp.int32, (w, LANES), 1)

    for hh in range(hb):
        cs = slice(hh * HEAD_DIM_A, (hh + 1) * HEAD_DIM_A)
        head = hblk * hb + hh
        for qb in range(nq):
            rs = slice(qb * w, (qb + 1) * w)
            q = q_ref[rs, cs]
            if qb == 0:
                kcat = jnp.concatenate([kp_ref[:, cs], k_ref[0:w, cs]], axis=0)
                vcat = jnp.concatenate([vp_ref[:, cs], v_ref[0:w, cs]], axis=0)
                mask = band_first
            else:
                kcat = k_ref[(qb - 1) * w:(qb + 1) * w, cs]
                vcat = v_ref[(qb - 1) * w:(qb + 1) * w, cs]
                mask = band
            s = lax.dot_general(q, kcat, (((1,), (1,)), ((), ())),
                                preferred_element_type=F32) * scale
            s = jnp.where(mask, s, -jnp.inf)
            m = jnp.max(s, axis=-1, keepdims=True)
            p = jnp.exp(s - m)
            den = jnp.sum(p, axis=-1, keepdims=True)
            o = jnp.dot(p.astype(BF16), vcat, preferred_element_type=F32) / den
            o_ref[rs, cs] = o.astype(o_ref.dtype)
            lse = m + jnp.log(den)
            lse_ref[rs, :] = jnp.where(lane == head, lse, lse_ref[rs, :])


def _dilated_attention(qkv, dilation):
    b, d, sd, _ = qkv.shape
    s = sd * d
    tq = min(ATTN_ROW_TILE, sd)
    hb = ATTN_HEADS_PER_STEP
    cw = hb * HEAD_DIM_A
    ncb = WIDTH_A // cw
    w = ATTN_STEPS
    rpt = tq // w

    def prev_map(section):
        return lambda bi, r, n, h: (bi, r, jnp.maximum(n * rpt - 1, 0), section * ncb + h)

    o, lse = pl.pallas_call(
        functools.partial(_attn_kernel, tq=tq, hb=hb),
        out_shape=(jax.ShapeDtypeStruct((b, sd, d * WIDTH_A), BF16),
                   jax.ShapeDtypeStruct((b, sd, d * LANES), F32)),
        grid=(b, d, sd // tq, ncb),
        in_specs=[
            pl.BlockSpec((None, None, tq, cw), lambda bi, r, n, h: (bi, r, n, h)),
            pl.BlockSpec((None, None, tq, cw), lambda bi, r, n, h: (bi, r, n, ncb + h)),
            pl.BlockSpec((None, None, tq, cw), lambda bi, r, n, h: (bi, r, n, 2 * ncb + h)),
            pl.BlockSpec((None, None, w, cw), prev_map(1)),
            pl.BlockSpec((None, None, w, cw), prev_map(2)),
        ],
        out_specs=(
            pl.BlockSpec((None, tq, cw), lambda bi, r, n, h: (bi, n, r * ncb + h)),
            pl.BlockSpec((None, tq, LANES), lambda bi, r, n, h: (bi, n, r)),
        ),
        compiler_params=_params(4),
    )(qkv, qkv, qkv, qkv, qkv)
    return o.reshape(b * s, WIDTH_A), lse.reshape(b * s, LANES)


def _merge_groups(o_refs, lse_refs, a_ref):
    ls = [r[...] for r in lse_refs]
    m = functools.reduce(jnp.maximum, ls)
    es = [jnp.exp(l - m) for l in ls]
    tot = functools.reduce(lambda u, v: u + v, es)
    alphas = [e / tot for e in es]
    for h in range(N_HEADS_A):
        cs = slice(h * HEAD_DIM_A, (h + 1) * HEAD_DIM_A)
        acc = None
        for al, o_ref in zip(alphas, o_refs):
            term = al[:, h:h + 1] * o_ref[:, cs].astype(F32)
            acc = term if acc is None else acc + term
        a_ref[:, cs] = acc.astype(a_ref.dtype)


def _attn_out_kernel(o0, o1, o2, l0, l1, l2, w_ref, x_ref, gate_ref, out_ref, a_ref):
    @pl.when(pl.program_id(1) == 0)
    def _():
        _merge_groups((o0, o1, o2), (l0, l1, l2), a_ref)

    y = jnp.dot(a_ref[...], w_ref[...], preferred_element_type=F32)
    out_ref[...] = x_ref[...] + gate_ref[...] * y


def _plain_out_kernel(a_ref, w_ref, x_ref, gate_ref, out_ref):
    y = jnp.dot(a_ref[...], w_ref[...], preferred_element_type=F32)
    out_ref[...] = x_ref[...] + gate_ref[...] * y


def _out_project(x2, gate, w, seq, acts, lses=None):
    m, d = x2.shape
    k = w.shape[0]
    tm, tn = ROW_TILE, OUT_COL_TILE
    tiles_per_seq = seq // tm
    row = lambda i, j: (i, 0)
    in_specs = [pl.BlockSpec((tm, k), row) for _ in acts]
    args = list(acts)
    scratch = []
    kern = _plain_out_kernel
    if lses is not None:
        in_specs += [pl.BlockSpec((tm, LANES), row) for _ in lses]
        args += list(lses)
        scratch = [pltpu.VMEM((tm, k), BF16)]
        kern = _attn_out_kernel
    in_specs += [
        pl.BlockSpec((k, tn), lambda i, j: (0, j)),
        pl.BlockSpec((tm, tn), lambda i, j: (i, j)),
        pl.BlockSpec((None, 1, tn), lambda i, j: (i // tiles_per_seq, 0, j)),
    ]
    args += [w, x2, gate]
    return pl.pallas_call(
        kern,
        out_shape=jax.ShapeDtypeStruct((m, d), F32),
        grid=(m // tm, d // tn),
        in_specs=in_specs,
        out_specs=pl.BlockSpec((tm, tn), lambda i, j: (i, j)),
        scratch_shapes=scratch,
        compiler_params=_params(2),
    )(*args)


def _retention_kernel(q_ref, k_ref, v_ref, g_ref, cos_ref, sin_ref, decay_ref, xi_ref,
                      zeta_ref, gamc_ref, gn_ref, o_ref, r_ref, *, tc, dk):
    c = CHUNK_B
    half = dk // 2
    kscale = dk ** -0.5

    @pl.when(pl.program_id(2) == 0)
    def _():
        r_ref[...] = jnp.zeros_like(r_ref)

    decay = decay_ref[...]
    xi = xi_ref[...]
    zeta = zeta_ref[...]
    gamc = gamc_ref[...]
    gn = gn_ref[...]

    def rot(x, cos, sin):
        x1, x2 = x[:, :half], x[:, half:]
        return jnp.concatenate([x1 * cos - x2 * sin, x1 * sin + x2 * cos], axis=-1)

    for ci in range(tc // c):
        rs = slice(ci * c, (ci + 1) * c)
        cos, sin = cos_ref[rs, :], sin_ref[rs, :]
        q = rot(q_ref[rs, :].astype(F32), cos, sin).astype(BF16)
        k = (rot(k_ref[rs, :].astype(F32), cos, sin) * kscale).astype(BF16)
        v = v_ref[rs, :]
        s = lax.dot_general(q, k, (((1,), (1,)), ((), ())), preferred_element_type=F32) * decay
        inner = jnp.dot(s.astype(BF16), v, preferred_element_type=F32)
        r = r_ref[...]
        cross = jnp.dot(q, r.astype(BF16), preferred_element_type=F32) * xi
        vz = (v.astype(F32) * zeta).astype(BF16)
        kv = lax.dot_general(k, vz, (((0,), (0,)), ((), ())), preferred_element_type=F32)
        r_ref[...] = gamc * r + kv
        y = inner + cross
        mu = jnp.mean(y, axis=-1, keepdims=True)
        yc = y - mu
        var = jnp.mean(yc * yc, axis=-1, keepdims=True)
        yn = yc * lax.rsqrt(var + EPS) * gn
        g = g_ref[rs, :].astype(F32)
        o_ref[rs, :] = (g * jax.nn.sigmoid(g) * yn).astype(o_ref.dtype)


def _retention_tables(seq, dk):
    h, c = N_HEADS_B, CHUNK_B
    half = dk // 2
    pos = jnp.arange(seq, dtype=F32)
    freqs = ROPE_BASE ** (-jnp.arange(half, dtype=F32) / half)
    ang = pos[:, None] * freqs[None, :]
    log_gamma = jnp.log1p(-jnp.exp2(-5.0 - jnp.arange(h, dtype=F32)))
    idx = jnp.arange(c, dtype=F32)
    rel = idx[:, None] - idx[None, :]
    decay = jnp.where(rel >= 0, jnp.exp(log_gamma[:, None, None] * jnp.maximum(rel, 0.0)), 0.0)
    xi = jnp.exp(log_gamma[:, None] * (idx + 1.0))[:, :, None]
    zeta = jnp.exp(log_gamma[:, None] * (c - 1.0 - idx))[:, :, None]
    gamc = jnp.exp(log_gamma * c)[:, None, None]
    return jnp.cos(ang), jnp.sin(ang), decay, xi, zeta, gamc


def _retention(proj, gn_g):
    b, s, n = proj.shape
    h = N_HEADS_B
    dv = gn_g.shape[-1] // h
    dk = (n - 2 * h * dv) // (2 * h)
    tc = RET_ROW_TILE
    c = CHUNK_B
    cos, sin, decay, xi, zeta, gamc = _retention_tables(s, dk)
    v_blk0 = (2 * h * dk) // dv
    out = pl.pallas_call(
        functools.partial(_retention_kernel, tc=tc, dk=dk),
        out_shape=jax.ShapeDtypeStruct((b, s, h * dv), BF16),
        grid=(b, h, s // tc),
        in_specs=[
            pl.BlockSpec((None, tc, dk), lambda bi, hi, t: (bi, t, hi)),
            pl.BlockSpec((None, tc, dk), lambda bi, hi, t: (bi, t, h + hi)),
            pl.BlockSpec((None, tc, dv), lambda bi, hi, t: (bi, t, v_blk0 + hi)),
            pl.BlockSpec((None, tc, dv), lambda bi, hi, t: (bi, t, v_blk0 + h + hi)),
            pl.BlockSpec((tc, dk // 2), lambda bi, hi, t: (t, 0)),
            pl.BlockSpec((tc, dk // 2), lambda bi, hi, t: (t, 0)),
            pl.BlockSpec((None, c, c), lambda bi, hi, t: (hi, 0, 0)),
            pl.BlockSpec((None, c, 1), lambda bi, hi, t: (hi, 0, 0)),
            pl.BlockSpec((None, c, 1), lambda bi, hi, t: (hi, 0, 0)),
            pl.BlockSpec((None, 1, 1), lambda bi, hi, t: (hi, 0, 0)),
            pl.BlockSpec((1, dv), lambda bi, hi, t: (0, hi)),
        ],
        out_specs=pl.BlockSpec((None, tc, dv), lambda bi, hi, t: (bi, t, hi)),
        scratch_shapes=[pltpu.VMEM((dk, dv), F32)],
        compiler_params=_params(3),
    )(proj, proj, proj, proj, cos, sin, decay, xi, zeta, gamc, gn_g.reshape(1, h * dv))
    return out.reshape(b * s, h * dv)


def _ffn_kernel(x_ref, xh_ref, gain_ref, shift_ref, scale_ref, gate_ref, wa_ref, wb_ref,
                cwa_ref, cwb_ref, cba_ref, cbb_ref, wd_ref, fg_ref, o_ref, h_ref, acc_ref,
                *, tiles_per_seq, final_norm):
    i = pl.program_id(0)
    f = pl.program_id(1)
    halo = HALO_ROWS

    @pl.when(f == 0)
    def _():
        gain, shift, scale = gain_ref[...], shift_ref[...], scale_ref[...]
        h_ref[halo:, :] = _norm_mod(x_ref[...], gain, shift, scale).astype(BF16)
        keep = (i % tiles_per_seq != 0).astype(F32)
        h_ref[:halo, :] = (_norm_mod(xh_ref[...], gain, shift, scale) * keep).astype(BF16)
        acc_ref[...] = jnp.zeros_like(acc_ref)

    h = h_ref[...]

    def conv_branch(w_ref, cw_ref, cb_ref):
        u = jnp.dot(h, w_ref[...], preferred_element_type=F32)
        cw = cw_ref[...]
        y = cw[2:3, :] * u[halo:, :]
        y = y + cw[1:2, :] * pltpu.roll(u, 1, 0)[halo:, :]
        y = y + cw[0:1, :] * pltpu.roll(u, 2, 0)[halo:, :]
        return y + cb_ref[...]

    a = conv_branch(wa_ref, cwa_ref, cba_ref)
    b = conv_branch(wb_ref, cwb_ref, cbb_ref)
    act = (a * jax.nn.sigmoid(a) * b).astype(BF16)
    acc_ref[...] += jnp.dot(act, wd_ref[...], preferred_element_type=F32)

    @pl.when(f == pl.num_programs(1) - 1)
    def _():
        xn = x_ref[...] + gate_ref[...] * acc_ref[...]
        if final_norm:
            xn = _rmsnorm(xn, fg_ref[...])
        o_ref[...] = xn


def _conv_ffn(x2, seq, gain, shift, scale, gate, w_up, conv_w, conv_b, w_down, final_gain,
              final_norm):
    m, d = x2.shape
    ff = w_down.shape[0]
    tm, tf = ROW_TILE, FFN_COL_TILE
    nf = ff // tf
    tiles_per_seq = seq // tm
    halo_blocks = tm // HALO_ROWS
    batch = lambda i, f: (i // tiles_per_seq, 0, 0)
    const = lambda i, f: (0, 0)
    return pl.pallas_call(
        functools.partial(_ffn_kernel, tiles_per_seq=tiles_per_seq, final_norm=final_norm),
        out_shape=jax.ShapeDtypeStruct((m, d), F32),
        grid=(m // tm, nf),
        in_specs=[
            pl.BlockSpec((tm, d), lambda i, f: (i, 0)),
            pl.BlockSpec((HALO_ROWS, d), lambda i, f: (jnp.maximum(i * halo_blocks - 1, 0), 0)),
            pl.BlockSpec((1, d), const),
            pl.BlockSpec((None, 1, d), batch),
            pl.BlockSpec((None, 1, d), batch),
            pl.BlockSpec((None, 1, d), batch),
            pl.BlockSpec((d, tf), lambda i, f: (0, f)),
            pl.BlockSpec((d, tf), lambda i, f: (0, nf + f)),
            pl.BlockSpec((CONV_WIDTH, tf), lambda i, f: (0, f)),
            pl.BlockSpec((CONV_WIDTH, tf), lambda i, f: (0, nf + f)),
            pl.BlockSpec((1, tf), lambda i, f: (0, f)),
            pl.BlockSpec((1, tf), lambda i, f: (0, nf + f)),
            pl.BlockSpec((tf, d), lambda i, f: (f, 0)),
            pl.BlockSpec((1, d), const),
        ],
        out_specs=pl.BlockSpec((tm, d), lambda i, f: (i, 0)),
        scratch_shapes=[pltpu.VMEM((tm + HALO_ROWS, d), BF16), pltpu.VMEM((tm, d), F32)],
        compiler_params=_params(2),
    )(x2, x2, gain, shift, scale, gate, w_up, w_up, conv_w, conv_w, conv_b, conv_b, w_down,
      final_gain)


def kernel(x, c, mod_w, mod_b, norm_mix, w_in_a, w_out_a, w_in_b, gn_b, w_out_b, norm_ffn,
           ffn_up, ffn_conv_w, ffn_conv_b, ffn_down, final_norm):
    b, s, d = x.shape
    depth = mod_w.shape[0]
    mod = _modulation(c, mod_w, mod_b).reshape(depth, b, 6, 1, d)
    x2 = x.reshape(b * s, d)
    group_cols = 3 * WIDTH_A

    for i in range(depth):
        sh_a, sc_a, g_a, sh_m, sc_m, g_m = (mod[i, :, k] for k in range(6))
        gain = norm_mix[i].reshape(1, d)
        x3 = x2.reshape(b, s, d)
        if i % 2 == 0:
            j = i // 2
            w_in = w_in_a[j].astype(BF16)
            outs, lses = [], []
            for g, (_, dil) in enumerate(DILATED_GROUPS):
                qkv = _project(x3, gain, sh_a, sc_a, w_in, g * group_cols, group_cols, dil)
                o, lse = _dilated_attention(qkv, dil)
                outs.append(o)
                lses.append(lse)
            x2 = _out_project(x2, g_a, w_out_a[j].astype(BF16), s, outs, lses)
        else:
            j = i // 2
            w_in = w_in_b[j].astype(BF16)
            proj = _project(x3, gain, sh_a, sc_a, w_in, 0, w_in.shape[1], 1)
            a = _retention(proj.reshape(b, s, -1), gn_b[j])
            x2 = _out_project(x2, g_a, w_out_b[j].astype(BF16), s, [a])
        x2 = _conv_ffn(
            x2, s, norm_ffn[i].reshape(1, d), sh_m, sc_m, g_m,
            ffn_up[i].astype(BF16), ffn_conv_w[i].reshape(CONV_WIDTH, -1),
            ffn_conv_b[i].reshape(1, -1), ffn_down[i].astype(BF16),
            final_norm.reshape(1, d), final_norm=(i == depth - 1))
    return x2.reshape(b, s, d)
```

```python
import functools
import math

import jax
import jax.numpy as jnp
from jax import lax
from jax.experimental import pallas as pl
from jax.experimental.pallas import tpu as pltpu

F32 = jnp.float32
BF16 = jnp.bfloat16

EPS = 1e-6
HEAD_DIM_A = 128
N_HEADS_A = 16
WIDTH_A = N_HEADS_A * HEAD_DIM_A
DILATED_GROUPS = ((128, 1), (512, 4), (2048, 16))
ATTN_STEPS = 128
N_HEADS_B = 8
CHUNK_B = 128
ROPE_BASE = 10000.0
CONV_WIDTH = 3

VMEM_LIMIT_BYTES = 56 * 1024 * 1024
LANES = 128
CONV_HALO_ROWS = 8

ROW_TILE = 512
PROJ_COL_TILE = 2048
OUT_COL_TILE = 1024
FFN_COL_TILE = 512
FFN_ROW_CHUNK = 256
ATTN_ROW_TILE = 512
ATTN_HEADS_PER_STEP = 4
RET_ROW_TILE = 512
MOD_COL_TILE = 1024


def _params(n_axes):
    return pltpu.CompilerParams(
        dimension_semantics=("arbitrary",) * n_axes,
        vmem_limit_bytes=VMEM_LIMIT_BYTES)


def _rmsnorm(x, gain):
    ms = jnp.mean(x * x, axis=-1, keepdims=True)
    return x * lax.rsqrt(ms + EPS) * gain


def _norm_mod(x, gain, shift, scale):
    return _rmsnorm(x, gain) * (1.0 + scale) + shift


def _mod_kernel(c_ref, w_ref, b_ref, o_ref):
    c = c_ref[...]
    c_act = (c * jax.nn.sigmoid(c)).astype(BF16)
    o_ref[...] = jnp.dot(c_act, w_ref[...].astype(BF16),
                         preferred_element_type=F32) + b_ref[...]


def _modulation(c, mod_w, mod_b):
    depth, d, n = mod_w.shape
    b = c.shape[0]
    tn = MOD_COL_TILE
    return pl.pallas_call(
        _mod_kernel,
        out_shape=jax.ShapeDtypeStruct((depth, b, n), F32),
        grid=(depth, n // tn),
        in_specs=[
            pl.BlockSpec((b, d), lambda l, j: (0, 0)),
            pl.BlockSpec((None, d, tn), lambda l, j: (l, 0, j)),
            pl.BlockSpec((None, 1, tn), lambda l, j: (l, 0, j)),
        ],
        out_specs=pl.BlockSpec((None, b, tn), lambda l, j: (l, 0, j)),
        compiler_params=_params(2),
    )(c, mod_w, mod_b.reshape(depth, 1, n))


def _proj_kernel(x_ref, gain_ref, shift_ref, scale_ref, w_ref, o_ref, h_ref):
    @pl.when(pl.program_id(3) == 0)
    def _():
        h_ref[...] = _norm_mod(x_ref[...], gain_ref[...], shift_ref[...],
                               scale_ref[...]).astype(BF16)

    o_ref[...] = jnp.dot(h_ref[...], w_ref[...],
                         preferred_element_type=F32).astype(o_ref.dtype)


def _project(x, gain, shift, scale, w, col0, n_cols, dilation):
    b, s, d = x.shape
    sd = s // dilation
    tm = min(ROW_TILE, sd)
    tn = PROJ_COL_TILE
    col_block0 = col0 // tn
    xv = x.reshape(b, sd, dilation * d)
    return pl.pallas_call(
        _proj_kernel,
        out_shape=jax.ShapeDtypeStruct((b, dilation, sd, n_cols), BF16),
        grid=(b, dilation, sd // tm, n_cols // tn),
        in_specs=[
            pl.BlockSpec((None, tm, d), lambda bi, r, n, j: (bi, n, r)),
            pl.BlockSpec((1, d), lambda bi, r, n, j: (0, 0)),
            pl.BlockSpec((None, 1, d), lambda bi, r, n, j: (bi, 0, 0)),
            pl.BlockSpec((None, 1, d), lambda bi, r, n, j: (bi, 0, 0)),
            pl.BlockSpec((d, tn), lambda bi, r, n, j: (0, col_block0 + j)),
        ],
        out_specs=pl.BlockSpec((None, None, tm, tn), lambda bi, r, n, j: (bi, r, n, j)),
        scratch_shapes=[pltpu.VMEM((tm, d), BF16)],
        compiler_params=_params(4),
    )(xv, gain, shift, scale, w)


def _attn_kernel(q_ref, k_ref, v_ref, kp_ref, vp_ref, o_ref, lse_ref, *, tq, hb):
    n = pl.program_id(2)
    hblk = pl.program_id(3)
    w = ATTN_STEPS
    nq = tq // w
    scale = HEAD_DIM_A ** -0.5

    @pl.when(hblk == 0)
    def _():
        lse_ref[...] = jnp.zeros_like(lse_ref)

    qi = lax.broadcasted_iota(jnp.int32, (w, 2 * w), 0)
    kj = lax.broadcasted_iota(jnp.int32, (w, 2 * w), 1)
    band = (kj >= qi) & (kj <= qi + w)
    band_first = band & ((kj >= w) | (n > 0))
    lane = lax.broadcasted_iota(jnp.int32, (w, LANES), 1)

    for hh in range(hb):
        cs = slice(hh * HEAD_DIM_A, (hh + 1) * HEAD_DIM_A)
        head = hblk * hb + hh
        for qb in range(nq):
            rs = slice(qb * w, (qb + 1) * w)
            q = q_ref[rs, cs]
            if qb == 0:
                kcat = jnp.concatenate([kp_ref[:, cs], k_ref[0:w, cs]], axis=0)
                vcat = jnp.concatenate([vp_ref[:, cs], v_ref[0:w, cs]], axis=0)
                mask = band_first
            else:
                kcat = k_ref[(qb - 1) * w:(qb + 1) * w, cs]
                vcat = v_ref[(qb - 1) * w:(qb + 1) * w, cs]
                mask = band
            s = lax.dot_general(q, kcat, (((1,), (1,)), ((), ())),
                                preferred_element_type=F32) * scale
            s = jnp.where(mask, s, -jnp.inf)
            m = jnp.max(s, axis=-1, keepdims=True)
            p = jnp.exp(s - m)
            den = jnp.sum(p, axis=-1, keepdims=True)
            o = jnp.dot(p.astype(BF16), vcat, preferred_element_type=F32) / den
            o_ref[rs, cs] = o.astype(o_ref.dtype)
            lse = m + jnp.log(den)
            lse_ref[rs, :] = jnp.where(lane == head, lse, lse_ref[rs, :])


def _dilated_attention(qkv, dilation):
    b, d, sd, _ = qkv.shape
    s = sd * d
    tq = min(ATTN_ROW_TILE, sd)
    hb = ATTN_HEADS_PER_STEP
    cw = hb * HEAD_DIM_A
    ncb = WIDTH_A // cw
    w = ATTN_STEPS
    rpt = tq // w

    def prev_map(section):
        return lambda bi, r, n, h: (bi, r, jnp.maximum(n * rpt - 1, 0), section * ncb + h)

    o, lse = pl.pallas_call(
        functools.partial(_attn_kernel, tq=tq, hb=hb),
        out_shape=(jax.ShapeDtypeStruct((b, sd, d * WIDTH_A), BF16),
                   jax.ShapeDtypeStruct((b, sd, d * LANES), F32)),
        grid=(b, d, sd // tq, ncb),
        in_specs=[
            pl.BlockSpec((None, None, tq, cw), lambda bi, r, n, h: (bi, r, n, h)),
            pl.BlockSpec((None, None, tq, cw), lambda bi, r, n, h: (bi, r, n, ncb + h)),
            pl.BlockSpec((None, None, tq, cw), lambda bi, r, n, h: (bi, r, n, 2 * ncb + h)),
            pl.BlockSpec((None, None, w, cw), prev_map(1)),
            pl.BlockSpec((None, None, w, cw), prev_map(2)),
        ],
        out_specs=(
            pl.BlockSpec((None, tq, cw), lambda bi, r, n, h: (bi, n, r * ncb + h)),
            pl.BlockSpec((None, tq, LANES), lambda bi, r, n, h: (bi, n, r)),
        ),
        compiler_params=_params(4),
    )(qkv, qkv, qkv, qkv, qkv)
    return o.reshape(b * s, WIDTH_A), lse.reshape(b * s, LANES)


def _merge_groups(o_refs, lse_refs, a_ref):
    ls = [r[...] for r in lse_refs]
    m = functools.reduce(jnp.maximum, ls)
    es = [jnp.exp(l - m) for l in ls]
    tot = functools.reduce(lambda u, v: u + v, es)
    alphas = [e / tot for e in es]
    for h in range(N_HEADS_A):
        cs = slice(h * HEAD_DIM_A, (h + 1) * HEAD_DIM_A)
        acc = None
        for al, o_ref in zip(alphas, o_refs):
            term = al[:, h:h + 1] * o_ref[:, cs].astype(F32)
            acc = term if acc is None else acc + term
        a_ref[:, cs] = acc.astype(a_ref.dtype)


def _attn_out_kernel(o0, o1, o2, l0, l1, l2, w_ref, x_ref, gate_ref, out_ref, a_ref):
    @pl.when(pl.program_id(1) == 0)
    def _():
        _merge_groups((o0, o1, o2), (l0, l1, l2), a_ref)

    y = jnp.dot(a_ref[...], w_ref[...], preferred_element_type=F32)
    out_ref[...] = x_ref[...] + gate_ref[...] * y


def _plain_out_kernel(a_ref, w_ref, x_ref, gate_ref, out_ref):
    y = jnp.dot(a_ref[...], w_ref[...], preferred_element_type=F32)
    out_ref[...] = x_ref[...] + gate_ref[...] * y


def _out_project(x2, gate, w, seq, acts, lses=None):
    m, d = x2.shape
    k = w.shape[0]
    tm, tn = ROW_TILE, OUT_COL_TILE
    tiles_per_seq = seq // tm
    row = lambda i, j: (i, 0)
    in_specs = [pl.BlockSpec((tm, k), row) for _ in acts]
    args = list(acts)
    scratch = []
    kern = _plain_out_kernel
    if lses is not None:
        in_specs += [pl.BlockSpec((tm, LANES), row) for _ in lses]
        args += list(lses)
        scratch = [pltpu.VMEM((tm, k), BF16)]
        kern = _attn_out_kernel
    in_specs += [
        pl.BlockSpec((k, tn), lambda i, j: (0, j)),
        pl.BlockSpec((tm, tn), lambda i, j: (i, j)),
        pl.BlockSpec((None, 1, tn), lambda i, j: (i // tiles_per_seq, 0, j)),
    ]
    args += [w, x2, gate]
    return pl.pallas_call(
        kern,
        out_shape=jax.ShapeDtypeStruct((m, d), F32),
        grid=(m // tm, d // tn),
        in_specs=in_specs,
        out_specs=pl.BlockSpec((tm, tn), lambda i, j: (i, j)),
        scratch_shapes=scratch,
        compiler_params=_params(2),
    )(*args)


def _retention_kernel(q_ref, k_ref, v_ref, g_ref, cos_ref, sin_ref, decay_ref, xi_ref,
                      zeta_ref, gamc_ref, gn_ref, o_ref, r_ref, *, tc, dk):
    c = CHUNK_B
    half = dk // 2
    kscale = dk ** -0.5

    @pl.when(pl.program_id(2) == 0)
    def _():
        r_ref[...] = jnp.zeros_like(r_ref)

    decay = decay_ref[...]
    xi = xi_ref[...]
    zeta = zeta_ref[...]
    gamc = gamc_ref[...]
    gn = gn_ref[...]

    def rot(x, cos, sin):
        x1, x2 = x[:, :half], x[:, half:]
        return jnp.concatenate([x1 * cos - x2 * sin, x1 * sin + x2 * cos], axis=-1)

    for ci in range(tc // c):
        rs = slice(ci * c, (ci + 1) * c)
        cos, sin = cos_ref[rs, :], sin_ref[rs, :]
        q = rot(q_ref[rs, :].astype(F32), cos, sin).astype(BF16)
        k = (rot(k_ref[rs, :].astype(F32), cos, sin) * kscale).astype(BF16)
        v = v_ref[rs, :]
        s = lax.dot_general(q, k, (((1,), (1,)), ((), ())), preferred_element_type=F32) * decay
        inner = jnp.dot(s.astype(BF16), v, preferred_element_type=F32)
        r = r_ref[...]
        cross = jnp.dot(q, r.astype(BF16), preferred_element_type=F32) * xi
        vz = (v.astype(F32) * zeta).astype(BF16)
        kv = lax.dot_general(k, vz, (((0,), (0,)), ((), ())), preferred_element_type=F32)
        r_ref[...] = gamc * r + kv
        y = inner + cross
        mu = jnp.mean(y, axis=-1, keepdims=True)
        yc = y - mu
        var = jnp.mean(yc * yc, axis=-1, keepdims=True)
        yn = yc * lax.rsqrt(var + EPS) * gn
        g = g_ref[rs, :].astype(F32)
        o_ref[rs, :] = (g * jax.nn.sigmoid(g) * yn).astype(o_ref.dtype)


def _retention_tables(seq, dk):
    h, c = N_HEADS_B, CHUNK_B
    half = dk // 2
    pos = jnp.arange(seq, dtype=F32)
    freqs = ROPE_BASE ** (-jnp.arange(half, dtype=F32) / half)
    ang = pos[:, None] * freqs[None, :]
    log_gamma = jnp.log1p(-jnp.exp2(-5.0 - jnp.arange(h, dtype=F32)))
    idx = jnp.arange(c, dtype=F32)
    rel = idx[:, None] - idx[None, :]
    decay = jnp.where(rel >= 0, jnp.exp(log_gamma[:, None, None] * jnp.maximum(rel, 0.0)), 0.0)
    xi = jnp.exp(log_gamma[:, None] * (idx + 1.0))[:, :, None]
    zeta = jnp.exp(log_gamma[:, None] * (c - 1.0 - idx))[:, :, None]
    gamc = jnp.exp(log_gamma * c)[:, None, None]
    return jnp.cos(ang), jnp.sin(ang), decay, xi, zeta, gamc


def _retention(proj, gn_g):
    b, s, n = proj.shape
    h = N_HEADS_B
    dv = gn_g.shape[-1] // h
    dk = (n - 2 * h * dv) // (2 * h)
    tc = RET_ROW_TILE
    c = CHUNK_B
    cos, sin, decay, xi, zeta, gamc = _retention_tables(s, dk)
    v_blk0 = (2 * h * dk) // dv
    out = pl.pallas_call(
        functools.partial(_retention_kernel, tc=tc, dk=dk),
        out_shape=jax.ShapeDtypeStruct((b, s, h * dv), BF16),
        grid=(b, h, s // tc),
        in_specs=[
            pl.BlockSpec((None, tc, dk), lambda bi, hi, t: (bi, t, hi)),
            pl.BlockSpec((None, tc, dk), lambda bi, hi, t: (bi, t, h + hi)),
            pl.BlockSpec((None, tc, dv), lambda bi, hi, t: (bi, t, v_blk0 + hi)),
            pl.BlockSpec((None, tc, dv), lambda bi, hi, t: (bi, t, v_blk0 + h + hi)),
            pl.BlockSpec((tc, dk // 2), lambda bi, hi, t: (t, 0)),
            pl.BlockSpec((tc, dk // 2), lambda bi, hi, t: (t, 0)),
            pl.BlockSpec((None, c, c), lambda bi, hi, t: (hi, 0, 0)),
            pl.BlockSpec((None, c, 1), lambda bi, hi, t: (hi, 0, 0)),
            pl.BlockSpec((None, c, 1), lambda bi, hi, t: (hi, 0, 0)),
            pl.BlockSpec((None, 1, 1), lambda bi, hi, t: (hi, 0, 0)),
            pl.BlockSpec((1, dv), lambda bi, hi, t: (0, hi)),
        ],
        out_specs=pl.BlockSpec((None, tc, dv), lambda bi, hi, t: (bi, t, hi)),
        scratch_shapes=[pltpu.VMEM((dk, dv), F32)],
        compiler_params=_params(3),
    )(proj, proj, proj, proj, cos, sin, decay, xi, zeta, gamc, gn_g.reshape(1, h * dv))
    return out.reshape(b * s, h * dv)


def _ffn_kernel(x_ref, gain_ref, shift_ref, scale_ref, gate_ref, wa_ref, wb_ref,
                cwa_ref, cwb_ref, cba_ref, cbb_ref, wd_ref, fg_ref, o_ref, h_ref, acc_ref,
                ua_ref, ub_ref, carry_ref, *, tiles_per_seq, final_norm):
    i = pl.program_id(0)
    f = pl.program_id(1)
    pad = CONV_HALO_ROWS
    tm = x_ref.shape[0]
    u_refs = (ua_ref, ub_ref)

    @pl.when(f == 0)
    def _():
        h_ref[...] = _norm_mod(x_ref[...], gain_ref[...], shift_ref[...],
                               scale_ref[...]).astype(BF16)
        acc_ref[...] = jnp.zeros_like(acc_ref)

    @pl.when(i % tiles_per_seq == 0)
    def _():
        for u_ref in u_refs:
            u_ref[0:pad, :] = jnp.zeros((pad, u_ref.shape[1]), F32)

    @pl.when(i % tiles_per_seq != 0)
    def _():
        for br, u_ref in enumerate(u_refs):
            u_ref[0:pad, :] = carry_ref[f, br]

    def up_project(r0, r1):
        hr = h_ref[r0:r1, :]
        for u_ref, w_ref in zip(u_refs, (wa_ref, wb_ref)):
            u_ref[pad + r0:pad + r1, :] = jnp.dot(hr, w_ref[...], preferred_element_type=F32)

    def conv(u_ref, cw_ref, cb_ref, r0, r1):
        cw = cw_ref[...]
        y = cw[2:3, :] * u_ref[pad + r0:pad + r1, :]
        y = y + cw[1:2, :] * u_ref[pad - 1 + r0:pad - 1 + r1, :]
        y = y + cw[0:1, :] * u_ref[pad - 2 + r0:pad - 2 + r1, :]
        return y + cb_ref[...]

    def down_project(r0, r1):
        a = conv(ua_ref, cwa_ref, cba_ref, r0, r1)
        b = conv(ub_ref, cwb_ref, cbb_ref, r0, r1)
        act = (a * jax.nn.sigmoid(a) * b).astype(BF16)
        acc_ref[r0:r1, :] += jnp.dot(act, wd_ref[...], preferred_element_type=F32)

    bounds = list(range(0, tm + 1, FFN_ROW_CHUNK))
    chunks = list(zip(bounds[:-1], bounds[1:]))
    up_project(*chunks[0])
    for ci, (r0, r1) in enumerate(chunks):
        if ci + 1 < len(chunks):
            up_project(*chunks[ci + 1])
        down_project(r0, r1)

    for br, u_ref in enumerate(u_refs):
        carry_ref[f, br] = u_ref[tm:tm + pad, :]

    @pl.when(f == pl.num_programs(1) - 1)
    def _():
        xn = x_ref[...] + gate_ref[...] * acc_ref[...]
        if final_norm:
            xn = _rmsnorm(xn, fg_ref[...])
        o_ref[...] = xn


def _conv_ffn(x2, seq, gain, shift, scale, gate, w_up, conv_w, conv_b, w_down, final_gain,
              layer, final_norm):
    m, d = x2.shape
    ff = w_down.shape[1]
    tm, tf = ROW_TILE, FFN_COL_TILE
    nf = ff // tf
    tiles_per_seq = seq // tm
    batch = lambda i, f: (i // tiles_per_seq, 0, 0)
    const = lambda i, f: (0, 0)
    return pl.pallas_call(
        functools.partial(_ffn_kernel, tiles_per_seq=tiles_per_seq, final_norm=final_norm),
        out_shape=jax.ShapeDtypeStruct((m, d), F32),
        grid=(m // tm, nf),
        in_specs=[
            pl.BlockSpec((tm, d), lambda i, f: (i, 0)),
            pl.BlockSpec((1, d), const),
            pl.BlockSpec((None, 1, d), batch),
            pl.BlockSpec((None, 1, d), batch),
            pl.BlockSpec((None, 1, d), batch),
            pl.BlockSpec((None, d, tf), lambda i, f: (layer, 0, f)),
            pl.BlockSpec((None, d, tf), lambda i, f: (layer, 0, nf + f)),
            pl.BlockSpec((None, CONV_WIDTH, tf), lambda i, f: (layer, 0, f)),
            pl.BlockSpec((None, CONV_WIDTH, tf), lambda i, f: (layer, 0, nf + f)),
            pl.BlockSpec((None, 1, tf), lambda i, f: (layer, 0, f)),
            pl.BlockSpec((None, 1, tf), lambda i, f: (layer, 0, nf + f)),
            pl.BlockSpec((None, tf, d), lambda i, f: (layer, f, 0)),
            pl.BlockSpec((1, d), const),
        ],
        out_specs=pl.BlockSpec((tm, d), lambda i, f: (i, 0)),
        scratch_shapes=[pltpu.VMEM((tm, d), BF16), pltpu.VMEM((tm, d), F32),
                        pltpu.VMEM((tm + CONV_HALO_ROWS, tf), F32),
                        pltpu.VMEM((tm + CONV_HALO_ROWS, tf), F32),
                        pltpu.VMEM((nf, 2, CONV_HALO_ROWS, tf), F32)],
        compiler_params=_params(2),
        name="conv_ffn",
    )(x2, gain, shift, scale, gate, w_up, w_up, conv_w, conv_w, conv_b, conv_b, w_down,
      final_gain)


def kernel(x, c, mod_w, mod_b, norm_mix, w_in_a, w_out_a, w_in_b, gn_b, w_out_b, norm_ffn,
           ffn_up, ffn_conv_w, ffn_conv_b, ffn_down, final_norm):
    b, s, d = x.shape
    depth = mod_w.shape[0]
    mod = _modulation(c, mod_w, mod_b).reshape(depth, b, 6, 1, d)
    x2 = x.reshape(b * s, d)
    group_cols = 3 * WIDTH_A
    ffn_up_bf = ffn_up.astype(BF16)
    ffn_down_bf = ffn_down.astype(BF16)
    conv_w = ffn_conv_w.reshape(depth, CONV_WIDTH, -1)
    conv_b = ffn_conv_b.reshape(depth, 1, -1)

    for i in range(depth):
        sh_a, sc_a, g_a, sh_m, sc_m, g_m = (mod[i, :, k] for k in range(6))
        gain = norm_mix[i].reshape(1, d)
        x3 = x2.reshape(b, s, d)
        if i % 2 == 0:
            j = i // 2
            w_in = w_in_a[j].astype(BF16)
            outs, lses = [], []
            for g, (_, dil) in enumerate(DILATED_GROUPS):
                qkv = _project(x3, gain, sh_a, sc_a, w_in, g * group_cols, group_cols, dil)
                o, lse = _dilated_attention(qkv, dil)
                outs.append(o)
                lses.append(lse)
            x2 = _out_project(x2, g_a, w_out_a[j].astype(BF16), s, outs, lses)
        else:
            j = i // 2
            w_in = w_in_b[j].astype(BF16)
            proj = _project(x3, gain, sh_a, sc_a, w_in, 0, w_in.shape[1], 1)
            a = _retention(proj.reshape(b, s, -1), gn_b[j])
            x2 = _out_project(x2, g_a, w_out_b[j].astype(BF16), s, [a])
        x2 = _conv_ffn(
            x2, s, norm_ffn[i].reshape(1, d), sh_m, sc_m, g_m, ffn_up_bf, conv_w, conv_b,
            ffn_down_bf, final_norm.reshape(1, d), layer=i, final_norm=(i == depth - 1))
    return x2.reshape(b, s, d)
```

```python
import functools

import jax
import jax.numpy as jnp
import numpy as np
from jax import lax
from jax.experimental import pallas as pl
from jax.experimental.pallas import tpu as pltpu

F32 = jnp.float32
BF16 = jnp.bfloat16

EPS = 1e-6
HEAD_DIM_A = 128
N_HEADS_A = 16
WIDTH_A = N_HEADS_A * HEAD_DIM_A
DILATED_GROUPS = ((128, 1), (512, 4), (2048, 16))
ATTN_STEPS = 128
N_HEADS_B = 8
CHUNK_B = 128
ROPE_BASE = 10000.0
CONV_WIDTH = 3

VMEM_LIMIT_BYTES = 56 * 1024 * 1024
LANES = 128
CONV_HALO_ROWS = 8

PERM_TILE = 512
PROJ_ROW_TILE = 1024
PROJ_COL_TILE = 1024
NORM_ROW_CHUNK = 16
NORM_UNROLL = 4
OUT_ROW_TILE = 1024
OUT_COL_TILE = 512
FFN_ROW_TILE = 1024
FFN_COL_TILE = 512
FFN_ROW_CHUNK = 256
ATTN_ROW_TILE = 512
ATTN_HEADS_PER_STEP = 4
RET_ROW_TILE = 512
MOD_COL_TILE = 1024


def _params(n_axes):
    return pltpu.CompilerParams(
        dimension_semantics=("arbitrary",) * n_axes,
        vmem_limit_bytes=VMEM_LIMIT_BYTES)


def _rmsnorm(x, gain):
    ms = jnp.mean(x * x, axis=-1, keepdims=True)
    return x * lax.rsqrt(ms + EPS) * gain


def _norm_mod_rows(x_ref, gain_ref, shift_ref, scale_ref, h_ref):
    gain = gain_ref[...]
    shift = shift_ref[...]
    scale1 = 1.0 + scale_ref[...]
    rows = NORM_ROW_CHUNK

    def body(k, carry):
        rs = pl.ds(pl.multiple_of(k * rows, rows), rows)
        h_ref[rs, :] = (_rmsnorm(x_ref[rs, :], gain) * scale1 + shift).astype(h_ref.dtype)
        return carry

    lax.fori_loop(0, x_ref.shape[0] // rows, body, 0, unroll=NORM_UNROLL)


def _class_major_perm(dilation):
    c = PERM_TILE // dilation
    p = np.arange(PERM_TILE)
    m = np.zeros((PERM_TILE, PERM_TILE), np.float32)
    m[p, (p % c) * dilation + p // c] = 1.0
    return m


def _mod_kernel(c_ref, w_ref, b_ref, o_ref):
    c = c_ref[...]
    c_act = (c * jax.nn.sigmoid(c)).astype(BF16)
    o_ref[...] = jnp.dot(c_act, w_ref[...].astype(BF16),
                         preferred_element_type=F32) + b_ref[...]


def _modulation(c, mod_w, mod_b):
    depth, d, n = mod_w.shape
    b = c.shape[0]
    tn = MOD_COL_TILE
    return pl.pallas_call(
        _mod_kernel,
        out_shape=jax.ShapeDtypeStruct((depth, b, n), F32),
        grid=(depth, n // tn),
        in_specs=[
            pl.BlockSpec((b, d), lambda l, j: (0, 0)),
            pl.BlockSpec((None, d, tn), lambda l, j: (l, 0, j)),
            pl.BlockSpec((None, 1, tn), lambda l, j: (l, 0, j)),
        ],
        out_specs=pl.BlockSpec((None, b, tn), lambda l, j: (l, 0, j)),
        compiler_params=_params(2),
        name="modulation",
    )(c, mod_w, mod_b.reshape(depth, 1, n))


def _proj_kernel(*refs, n_groups, col_blocks_per_group):
    if n_groups > 1:
        x_ref, gain_ref, shift_ref, scale_ref, perm_ref, w_ref, o_ref, h_ref = refs
    else:
        x_ref, gain_ref, shift_ref, scale_ref, w_ref, o_ref, h_ref = refs
    j = pl.program_id(1)
    tm = x_ref.shape[0]

    @pl.when(j == 0)
    def _():
        _norm_mod_rows(x_ref, gain_ref, shift_ref, scale_ref, h_ref.at[0])
        for g in range(1, n_groups):
            for t0 in range(0, tm, PERM_TILE):
                rs = slice(t0, t0 + PERM_TILE)
                h_ref[g, rs, :] = jnp.dot(perm_ref[g - 1], h_ref[0, rs, :],
                                          preferred_element_type=F32).astype(BF16)

    g = j // col_blocks_per_group
    o_ref[...] = jnp.dot(h_ref[g], w_ref[...],
                         preferred_element_type=F32).astype(o_ref.dtype)


def _project(x2, seq, gain, shift, scale, w, dilations):
    m, d = x2.shape
    n = w.shape[1]
    tm, tn = PROJ_ROW_TILE, PROJ_COL_TILE
    n_groups = len(dilations)
    tiles_per_seq = seq // tm
    batch = lambda i, j: (i // tiles_per_seq, 0, 0)
    in_specs = [
        pl.BlockSpec((tm, d), lambda i, j: (i, 0)),
        pl.BlockSpec((1, d), lambda i, j: (0, 0)),
        pl.BlockSpec((None, 1, d), batch),
        pl.BlockSpec((None, 1, d), batch),
    ]
    args = [x2, gain, shift, scale]
    if n_groups > 1:
        assert dilations[0] == 1
        perms = jnp.asarray(np.stack([_class_major_perm(dl) for dl in dilations[1:]]), BF16)
        in_specs.append(pl.BlockSpec(perms.shape, lambda i, j: (0, 0, 0)))
        args.append(perms)
    in_specs.append(pl.BlockSpec((d, tn), lambda i, j: (0, j)))
    args.append(w)
    return pl.pallas_call(
        functools.partial(_proj_kernel, n_groups=n_groups,
                          col_blocks_per_group=n // n_groups // tn),
        out_shape=jax.ShapeDtypeStruct((m, n), BF16),
        grid=(m // tm, n // tn),
        in_specs=in_specs,
        out_specs=pl.BlockSpec((tm, tn), lambda i, j: (i, j)),
        scratch_shapes=[pltpu.VMEM((n_groups, tm, d), BF16)],
        compiler_params=_params(2),
        name="in_project",
    )(*args)


def _load_rows(ref, r0, n, cs):
    c = ref.shape[1]
    if n <= c:
        return ref[r0 // c, r0 % c:r0 % c + n, cs]
    return jnp.concatenate([ref[k, :, cs] for k in range(r0 // c, (r0 + n) // c)], axis=0)


def _store_rows(ref, r0, cs, val):
    c = ref.shape[1]
    n = val.shape[0]
    if n <= c:
        ref[r0 // c, r0 % c:r0 % c + n, cs] = val
    else:
        for i, k in enumerate(range(r0 // c, (r0 + n) // c)):
            ref[k, :, cs] = val[i * c:(i + 1) * c]


def _attn_kernel(q_ref, k_ref, v_ref, kp_ref, vp_ref, o_ref, lse_ref, *, tq, hb):
    n = pl.program_id(2)
    hblk = pl.program_id(3)
    w = ATTN_STEPS
    nq = tq // w
    scale = HEAD_DIM_A ** -0.5
    all_lanes = slice(0, LANES)

    @pl.when(hblk == 0)
    def _():
        lse_ref[...] = jnp.zeros_like(lse_ref)

    qi = lax.broadcasted_iota(jnp.int32, (w, 2 * w), 0)
    kj = lax.broadcasted_iota(jnp.int32, (w, 2 * w), 1)
    band = (kj >= qi) & (kj <= qi + w)
    band_first = band & ((kj >= w) | (n > 0))
    lane = lax.broadcasted_iota(jnp.int32, (w, LANES), 1)

    for hh in range(hb):
        cs = slice(hh * HEAD_DIM_A, (hh + 1) * HEAD_DIM_A)
        head = hblk * hb + hh
        for qb in range(nq):
            q = _load_rows(q_ref, qb * w, w, cs)
            if qb == 0:
                kcat = jnp.concatenate([_load_rows(kp_ref, 0, w, cs),
                                        _load_rows(k_ref, 0, w, cs)], axis=0)
                vcat = jnp.concatenate([_load_rows(vp_ref, 0, w, cs),
                                        _load_rows(v_ref, 0, w, cs)], axis=0)
                mask = band_first
            else:
                kcat = _load_rows(k_ref, (qb - 1) * w, 2 * w, cs)
                vcat = _load_rows(v_ref, (qb - 1) * w, 2 * w, cs)
                mask = band
            s = lax.dot_general(q, kcat, (((1,), (1,)), ((), ())),
                                preferred_element_type=F32) * scale
            s = jnp.where(mask, s, -jnp.inf)
            m = jnp.max(s, axis=-1, keepdims=True)
            p = jnp.exp(s - m)
            den = jnp.sum(p, axis=-1, keepdims=True)
            o = jnp.dot(p.astype(BF16), vcat, preferred_element_type=F32) / den
            _store_rows(o_ref, qb * w, cs, o.astype(o_ref.dtype))
            lse = m + jnp.log(den)
            prev = _load_rows(lse_ref, qb * w, w, all_lanes)
            _store_rows(lse_ref, qb * w, all_lanes, jnp.where(lane == head, lse, prev))


def _dilated_attention(qkv, seq, group, dilation):
    m, n_all = qkv.shape
    b = m // seq
    d = dilation
    c = PERM_TILE // d
    tiles = seq // PERM_TILE
    tq = ATTN_ROW_TILE
    tpq = tq // c
    hb = ATTN_HEADS_PER_STEP
    cw = hb * HEAD_DIM_A
    ncb = WIDTH_A // cw
    w = ATTN_STEPS
    col0 = group * 3 * ncb
    pc = min(c, w)
    ptiles = w // pc
    rows_view = lambda a: a.reshape(b, tiles, d, c, a.shape[-1])

    def prev_map(section):
        if c >= w:
            return lambda bi, r, nn, h: (bi, jnp.maximum(nn * tpq - 1, 0), r, c // w - 1,
                                         col0 + section * ncb + h)
        return lambda bi, r, nn, h: (bi, jnp.maximum(nn * (tpq // ptiles) - 1, 0), r, 0,
                                     col0 + section * ncb + h)

    def cur_map(section):
        return lambda bi, r, nn, h: (bi, nn, r, 0, col0 + section * ncb + h)

    qv = rows_view(qkv)
    o, lse = pl.pallas_call(
        functools.partial(_attn_kernel, tq=tq, hb=hb),
        out_shape=(jax.ShapeDtypeStruct((b, tiles, d, c, WIDTH_A), BF16),
                   jax.ShapeDtypeStruct((b, tiles, d, c, LANES), F32)),
        grid=(b, d, seq // d // tq, ncb),
        in_specs=[
            pl.BlockSpec((None, tpq, None, c, cw), cur_map(0)),
            pl.BlockSpec((None, tpq, None, c, cw), cur_map(1)),
            pl.BlockSpec((None, tpq, None, c, cw), cur_map(2)),
            pl.BlockSpec((None, ptiles, None, pc, cw), prev_map(1)),
            pl.BlockSpec((None, ptiles, None, pc, cw), prev_map(2)),
        ],
        out_specs=(
            pl.BlockSpec((None, tpq, None, c, cw), lambda bi, r, nn, h: (bi, nn, r, 0, h)),
            pl.BlockSpec((None, tpq, None, c, LANES), lambda bi, r, nn, h: (bi, nn, r, 0, 0)),
        ),
        compiler_params=_params(4),
        name=f"dilated_attention_d{d}",
    )(qv, qv, qv, qv, qv)
    return o.reshape(m, WIDTH_A), lse.reshape(m, LANES)


def _merge_tile(o_refs, lse_refs, pt_ref, lsn_refs, a_ref, dilations):
    ls = [lse_refs[0][...]]
    for g in range(1, len(dilations)):
        d = dilations[g]
        c = PERM_TILE // d
        for r in range(d):
            lsn_refs[g - 1][pl.ds(r, c, stride=d), :] = lse_refs[g][r * c:(r + 1) * c, :]
        ls.append(lsn_refs[g - 1][...])
    mx = functools.reduce(jnp.maximum, ls)
    es = [jnp.exp(l - mx) for l in ls]
    inv = 1.0 / functools.reduce(lambda u, v: u + v, es)
    alphas = [e * inv for e in es[1:]]
    heads_per_dot = 2
    for hp in range(N_HEADS_A // heads_per_dot):
        cs2 = slice(hp * heads_per_dot * HEAD_DIM_A, (hp + 1) * heads_per_dot * HEAD_DIM_A)
        base = o_refs[0][:, cs2].astype(F32)
        others = [jnp.dot(pt_ref[g - 1], o_refs[g][:, cs2], preferred_element_type=F32)
                  for g in range(1, len(dilations))]
        for hh in range(heads_per_dot):
            h = hp * heads_per_dot + hh
            ls_ = slice(hh * HEAD_DIM_A, (hh + 1) * HEAD_DIM_A)
            acc = base[:, ls_]
            for al, og in zip(alphas, others):
                acc = acc + al[:, h:h + 1] * (og[:, ls_] - base[:, ls_])
            a_ref[:, h * HEAD_DIM_A:(h + 1) * HEAD_DIM_A] = acc.astype(a_ref.dtype)


def _attn_out_kernel(o0, o1, o2, l0, l1, l2, pt_ref, w_ref, x_ref, gate_ref, out_ref,
                     a0_ref, a1_ref, lsn1_ref, lsn2_ref, *, dilations):
    s = pl.program_id(0)

    @pl.when(s == 0)
    def _():
        a1_ref[...] = jnp.zeros_like(a1_ref)

    def step(a_new, a_old):
        _merge_tile((o0, o1, o2), (l0, l1, l2), pt_ref, (lsn1_ref, lsn2_ref), a_new, dilations)
        y = jnp.dot(a_old[...], w_ref[...], preferred_element_type=F32)
        out_ref[...] = x_ref[...] + gate_ref[...] * y

    @pl.when(s % 2 == 0)
    def _():
        step(a0_ref, a1_ref)

    @pl.when(s % 2 == 1)
    def _():
        step(a1_ref, a0_ref)


def _attn_out_project(x2, seq, gate, w, outs, lses, dilations):
    m, d = x2.shape
    k = w.shape[0]
    tm = PERM_TILE
    n_tiles = m // tm
    tiles_per_seq = seq // tm
    pts = jnp.asarray(np.stack([_class_major_perm(dl).T for dl in dilations[1:]]), BF16)
    cur = lambda s: (jnp.minimum(s, n_tiles - 1), 0)
    prev = lambda s: (jnp.maximum(s - 1, 0), 0)
    in_specs = ([pl.BlockSpec((tm, k), cur) for _ in outs]
                + [pl.BlockSpec((tm, LANES), cur) for _ in lses]
                + [pl.BlockSpec(pts.shape, lambda s: (0, 0, 0)),
                   pl.BlockSpec((k, d), lambda s: (0, 0), pipeline_mode=pl.Buffered(1)),
                   pl.BlockSpec((tm, d), prev),
                   pl.BlockSpec((None, 1, d),
                                lambda s: (jnp.maximum(s - 1, 0) // tiles_per_seq, 0, 0))])
    return pl.pallas_call(
        functools.partial(_attn_out_kernel, dilations=dilations),
        out_shape=jax.ShapeDtypeStruct((m, d), F32),
        grid=(n_tiles + 1,),
        in_specs=in_specs,
        out_specs=pl.BlockSpec((tm, d), prev),
        scratch_shapes=[pltpu.VMEM((tm, k), BF16), pltpu.VMEM((tm, k), BF16),
                        pltpu.VMEM((tm, LANES), F32), pltpu.VMEM((tm, LANES), F32)],
        compiler_params=_params(1),
        name="attn_out_project",
    )(*outs, *lses, pts, w, x2, gate)


def _out_kernel(a_ref, w_ref, x_ref, gate_ref, out_ref):
    y = jnp.dot(a_ref[...], w_ref[...], preferred_element_type=F32)
    out_ref[...] = x_ref[...] + gate_ref[...] * y


def _out_project(x2, seq, gate, w, a):
    m, d = x2.shape
    k = w.shape[0]
    tm, tn = OUT_ROW_TILE, OUT_COL_TILE
    tiles_per_seq = seq // tm
    return pl.pallas_call(
        _out_kernel,
        out_shape=jax.ShapeDtypeStruct((m, d), F32),
        grid=(m // tm, d // tn),
        in_specs=[
            pl.BlockSpec((tm, k), lambda i, j: (i, 0)),
            pl.BlockSpec((k, tn), lambda i, j: (0, j)),
            pl.BlockSpec((tm, tn), lambda i, j: (i, j)),
            pl.BlockSpec((None, 1, tn), lambda i, j: (i // tiles_per_seq, 0, j)),
        ],
        out_specs=pl.BlockSpec((tm, tn), lambda i, j: (i, j)),
        compiler_params=_params(2),
        name="out_project",
    )(a, w, x2, gate)


def _retention_kernel(q_ref, k_ref, v_ref, g_ref, cos_ref, sin_ref, decay_ref, xi_ref,
                      zeta_ref, gamc_ref, gn_ref, o_ref, r_ref, *, tc, dk):
    c = CHUNK_B
    half = dk // 2
    kscale = dk ** -0.5

    @pl.when(pl.program_id(2) == 0)
    def _():
        r_ref[...] = jnp.zeros_like(r_ref)

    decay = decay_ref[...]
    xi = xi_ref[...]
    zeta = zeta_ref[...]
    gamc = gamc_ref[...]
    gn = gn_ref[...]

    def rot(x, cos, sin):
        x1, x2 = x[:, :half], x[:, half:]
        return jnp.concatenate([x1 * cos - x2 * sin, x1 * sin + x2 * cos], axis=-1)

    for ci in range(tc // c):
        rs = slice(ci * c, (ci + 1) * c)
        cos, sin = cos_ref[rs, :], sin_ref[rs, :]
        q = rot(q_ref[rs, :].astype(F32), cos, sin).astype(BF16)
        k = (rot(k_ref[rs, :].astype(F32), cos, sin) * kscale).astype(BF16)
        v = v_ref[rs, :]
        s = lax.dot_general(q, k, (((1,), (1,)), ((), ())), preferred_element_type=F32) * decay
        inner = jnp.dot(s.astype(BF16), v, preferred_element_type=F32)
        r = r_ref[...]
        cross = jnp.dot(q, r.astype(BF16), preferred_element_type=F32) * xi
        vz = (v.astype(F32) * zeta).astype(BF16)
        kv = lax.dot_general(k, vz, (((0,), (0,)), ((), ())), preferred_element_type=F32)
        r_ref[...] = gamc * r + kv
        y = inner + cross
        mu = jnp.mean(y, axis=-1, keepdims=True)
        yc = y - mu
        var = jnp.mean(yc * yc, axis=-1, keepdims=True)
        yn = yc * lax.rsqrt(var + EPS) * gn
        g = g_ref[rs, :].astype(F32)
        o_ref[rs, :] = (g * jax.nn.sigmoid(g) * yn).astype(o_ref.dtype)


def _retention_tables(seq, dk):
    h, c = N_HEADS_B, CHUNK_B
    half = dk // 2
    pos = jnp.arange(seq, dtype=F32)
    freqs = ROPE_BASE ** (-jnp.arange(half, dtype=F32) / half)
    ang = pos[:, None] * freqs[None, :]
    log_gamma = jnp.log1p(-jnp.exp2(-5.0 - jnp.arange(h, dtype=F32)))
    idx = jnp.arange(c, dtype=F32)
    rel = idx[:, None] - idx[None, :]
    decay = jnp.where(rel >= 0, jnp.exp(log_gamma[:, None, None] * jnp.maximum(rel, 0.0)), 0.0)
    xi = jnp.exp(log_gamma[:, None] * (idx + 1.0))[:, :, None]
    zeta = jnp.exp(log_gamma[:, None] * (c - 1.0 - idx))[:, :, None]
    gamc = jnp.exp(log_gamma * c)[:, None, None]
    return jnp.cos(ang), jnp.sin(ang), decay, xi, zeta, gamc


def _retention(proj, gn_g):
    b, s, n = proj.shape
    h = N_HEADS_B
    dv = gn_g.shape[-1] // h
    dk = (n - 2 * h * dv) // (2 * h)
    tc = RET_ROW_TILE
    c = CHUNK_B
    cos, sin, decay, xi, zeta, gamc = _retention_tables(s, dk)
    v_blk0 = (2 * h * dk) // dv
    out = pl.pallas_call(
        functools.partial(_retention_kernel, tc=tc, dk=dk),
        out_shape=jax.ShapeDtypeStruct((b, s, h * dv), BF16),
        grid=(b, h, s // tc),
        in_specs=[
            pl.BlockSpec((None, tc, dk), lambda bi, hi, t: (bi, t, hi)),
            pl.BlockSpec((None, tc, dk), lambda bi, hi, t: (bi, t, h + hi)),
            pl.BlockSpec((None, tc, dv), lambda bi, hi, t: (bi, t, v_blk0 + hi)),
            pl.BlockSpec((None, tc, dv), lambda bi, hi, t: (bi, t, v_blk0 + h + hi)),
            pl.BlockSpec((tc, dk // 2), lambda bi, hi, t: (t, 0)),
            pl.BlockSpec((tc, dk // 2), lambda bi, hi, t: (t, 0)),
            pl.BlockSpec((None, c, c), lambda bi, hi, t: (hi, 0, 0)),
            pl.BlockSpec((None, c, 1), lambda bi, hi, t: (hi, 0, 0)),
            pl.BlockSpec((None, c, 1), lambda bi, hi, t: (hi, 0, 0)),
            pl.BlockSpec((None, 1, 1), lambda bi, hi, t: (hi, 0, 0)),
            pl.BlockSpec((1, dv), lambda bi, hi, t: (0, hi)),
        ],
        out_specs=pl.BlockSpec((None, tc, dv), lambda bi, hi, t: (bi, t, hi)),
        scratch_shapes=[pltpu.VMEM((dk, dv), F32)],
        compiler_params=_params(3),
        name="retention",
    )(proj, proj, proj, proj, cos, sin, decay, xi, zeta, gamc, gn_g.reshape(1, h * dv))
    return out.reshape(b * s, h * dv)


def _ffn_kernel(x_ref, gain_ref, shift_ref, scale_ref, gate_ref, wa_ref, wb_ref,
                cwa_ref, cwb_ref, cba_ref, cbb_ref, wd_ref, fg_ref, o_ref, h_ref,
                ua_ref, ub_ref, carry_ref, *, tiles_per_seq, final_norm):
    i = pl.program_id(0)
    f = pl.program_id(1)
    pad = CONV_HALO_ROWS
    tm = x_ref.shape[0]
    u_refs = (ua_ref, ub_ref)

    @pl.when(f == 0)
    def _():
        _norm_mod_rows(x_ref, gain_ref, shift_ref, scale_ref, h_ref)
        o_ref[...] = jnp.zeros_like(o_ref)

    @pl.when(i % tiles_per_seq == 0)
    def _():
        for u_ref in u_refs:
            u_ref[0:pad, :] = jnp.zeros((pad, u_ref.shape[1]), F32)

    @pl.when(i % tiles_per_seq != 0)
    def _():
        for br, u_ref in enumerate(u_refs):
            u_ref[0:pad, :] = carry_ref[f, br]

    def up_project(r0, r1):
        hr = h_ref[r0:r1, :]
        for u_ref, w_ref in zip(u_refs, (wa_ref, wb_ref)):
            u_ref[pad + r0:pad + r1, :] = jnp.dot(hr, w_ref[...], preferred_element_type=F32)

    def conv(u_ref, cw_ref, cb_ref, r0, r1):
        cw = cw_ref[...]
        y = cw[2:3, :] * u_ref[pad + r0:pad + r1, :]
        y = y + cw[1:2, :] * u_ref[pad - 1 + r0:pad - 1 + r1, :]
        y = y + cw[0:1, :] * u_ref[pad - 2 + r0:pad - 2 + r1, :]
        return y + cb_ref[...]

    def down_project(r0, r1):
        a = conv(ua_ref, cwa_ref, cba_ref, r0, r1)
        b = conv(ub_ref, cwb_ref, cbb_ref, r0, r1)
        act = (a * jax.nn.sigmoid(a) * b).astype(BF16)
        o_ref[r0:r1, :] += jnp.dot(act, wd_ref[...], preferred_element_type=F32)

    bounds = list(range(0, tm + 1, FFN_ROW_CHUNK))
    chunks = list(zip(bounds[:-1], bounds[1:]))
    up_project(*chunks[0])
    for ci, (r0, r1) in enumerate(chunks):
        if ci + 1 < len(chunks):
            up_project(*chunks[ci + 1])
        down_project(r0, r1)

    for br, u_ref in enumerate(u_refs):
        carry_ref[f, br] = u_ref[tm:tm + pad, :]

    @pl.when(f == pl.num_programs(1) - 1)
    def _():
        xn = x_ref[...] + gate_ref[...] * o_ref[...]
        if final_norm:
            xn = _rmsnorm(xn, fg_ref[...])
        o_ref[...] = xn


def _conv_ffn(x2, seq, gain, shift, scale, gate, w_up, conv_w, conv_b, w_down, final_gain,
              layer, final_norm):
    m, d = x2.shape
    ff = w_down.shape[1]
    tm, tf = FFN_ROW_TILE, FFN_COL_TILE
    nf = ff // tf
    tiles_per_seq = seq // tm
    batch = lambda i, f: (i // tiles_per_seq, 0, 0)
    const = lambda i, f: (0, 0)
    return pl.pallas_call(
        functools.partial(_ffn_kernel, tiles_per_seq=tiles_per_seq, final_norm=final_norm),
        out_shape=jax.ShapeDtypeStruct((m, d), F32),
        grid=(m // tm, nf),
        in_specs=[
            pl.BlockSpec((tm, d), lambda i, f: (i, 0), pipeline_mode=pl.Buffered(1)),
            pl.BlockSpec((1, d), const),
            pl.BlockSpec((None, 1, d), batch),
            pl.BlockSpec((None, 1, d), batch),
            pl.BlockSpec((None, 1, d), batch),
            pl.BlockSpec((None, d, tf), lambda i, f: (layer, 0, f)),
            pl.BlockSpec((None, d, tf), lambda i, f: (layer, 0, nf + f)),
            pl.BlockSpec((None, CONV_WIDTH, tf), lambda i, f: (layer, 0, f)),
            pl.BlockSpec((None, CONV_WIDTH, tf), lambda i, f: (layer, 0, nf + f)),
            pl.BlockSpec((None, 1, tf), lambda i, f: (layer, 0, f)),
            pl.BlockSpec((None, 1, tf), lambda i, f: (layer, 0, nf + f)),
            pl.BlockSpec((None, tf, d), lambda i, f: (layer, f, 0)),
            pl.BlockSpec((1, d), const),
        ],
        out_specs=pl.BlockSpec((tm, d), lambda i, f: (i, 0)),
        scratch_shapes=[pltpu.VMEM((tm, d), BF16),
                        pltpu.VMEM((tm + CONV_HALO_ROWS, tf), F32),
                        pltpu.VMEM((tm + CONV_HALO_ROWS, tf), F32),
                        pltpu.VMEM((nf, 2, CONV_HALO_ROWS, tf), F32)],
        compiler_params=_params(2),
        name="conv_ffn",
    )(x2, gain, shift, scale, gate, w_up, w_up, conv_w, conv_w, conv_b, conv_b, w_down,
      final_gain)


def kernel(x, c, mod_w, mod_b, norm_mix, w_in_a, w_out_a, w_in_b, gn_b, w_out_b, norm_ffn,
           ffn_up, ffn_conv_w, ffn_conv_b, ffn_down, final_norm):
    b, s, d = x.shape
    depth = mod_w.shape[0]
    mod = _modulation(c, mod_w, mod_b).reshape(depth, b, 6, 1, d)
    x2 = x.reshape(b * s, d)
    dilations = tuple(dl for _, dl in DILATED_GROUPS)
    ffn_up_bf = ffn_up.astype(BF16)
    ffn_down_bf = ffn_down.astype(BF16)
    conv_w = ffn_conv_w.reshape(depth, CONV_WIDTH, -1)
    conv_b = ffn_conv_b.reshape(depth, 1, -1)

    for i in range(depth):
        sh_a, sc_a, g_a, sh_m, sc_m, g_m = (mod[i, :, k] for k in range(6))
        gain = norm_mix[i].reshape(1, d)
        j = i // 2
        if i % 2 == 0:
            qkv = _project(x2, s, gain, sh_a, sc_a, w_in_a[j].astype(BF16), dilations)
            outs, lses = [], []
            for g, dil in enumerate(dilations):
                o, lse = _dilated_attention(qkv, s, g, dil)
                outs.append(o)
                lses.append(lse)
            x2 = _attn_out_project(x2, s, g_a, w_out_a[j].astype(BF16), outs, lses, dilations)
        else:
            proj = _project(x2, s, gain, sh_a, sc_a, w_in_b[j].astype(BF16), (1,))
            a = _retention(proj.reshape(b, s, -1), gn_b[j])
            x2 = _out_project(x2, s, g_a, w_out_b[j].astype(BF16), a)
        x2 = _conv_ffn(
            x2, s, norm_ffn[i].reshape(1, d), sh_m, sc_m, g_m, ffn_up_bf, conv_w, conv_b,
            ffn_down_bf, final_norm.reshape(1, d), layer=i, final_norm=(i == depth - 1))
    return x2.reshape(b, s, d)
```

```python
import functools

import jax
import jax.numpy as jnp
import numpy as np
from jax import lax
from jax.experimental import pallas as pl
from jax.experimental.pallas import tpu as pltpu

F32 = jnp.float32
BF16 = jnp.bfloat16

EPS = 1e-6
LOG2_E = 1.4426950408889634
HEAD_DIM_A = 128
N_HEADS_A = 16
WIDTH_A = N_HEADS_A * HEAD_DIM_A
DILATED_GROUPS = ((128, 1), (512, 4), (2048, 16))
ATTN_STEPS = 128
N_HEADS_B = 8
RET_CHUNK = 256
ROPE_BASE = 10000.0
CONV_WIDTH = 3

VMEM_LIMIT_BYTES = 60 * 1024 * 1024
LANES = 128
CONV_HALO_ROWS = 8

PERM_TILE = 512
PROJ_ROW_TILE = 1024
PROJ_COL_TILE = 1024
NORM_ROW_CHUNK = 16
NORM_UNROLL = 4
OUT_ROW_TILE = 1024
OUT_COL_TILE = 512
FFN_ROW_TILE = 1024
FFN_COL_TILE = 512
FFN_ROW_CHUNK = 512
ATTN_ROW_TILE = 1024
ATTN_BLOCKS_PER_STEP = 32
RET_ROW_TILE = 512
MOD_COL_TILE = 1024


def _params(n_axes):
    return pltpu.CompilerParams(
        dimension_semantics=("arbitrary",) * n_axes,
        vmem_limit_bytes=VMEM_LIMIT_BYTES)


def _silu(x):
    h = 0.5 * x
    return h + h * jnp.tanh(h)


def _rmsnorm(x, gain):
    ms = jnp.mean(x * x, axis=-1, keepdims=True)
    return x * lax.rsqrt(ms + EPS) * gain


def _norm_mod_rows(x_ref, gain_ref, shift_ref, scale_ref, h_ref):
    shift = shift_ref[...]
    gain_mod = gain_ref[...] * (1.0 + scale_ref[...])
    rows = NORM_ROW_CHUNK

    def body(k, carry):
        rs = pl.ds(pl.multiple_of(k * rows, rows), rows)
        h_ref[rs, :] = (_rmsnorm(x_ref[rs, :], gain_mod) + shift).astype(h_ref.dtype)
        return carry

    lax.fori_loop(0, x_ref.shape[0] // rows, body, 0, unroll=NORM_UNROLL)


def _class_major_perm(dilation):
    c = PERM_TILE // dilation
    p = np.arange(PERM_TILE)
    m = np.zeros((PERM_TILE, PERM_TILE), np.float32)
    m[p, (p % c) * dilation + p // c] = 1.0
    return m


def _mod_kernel(c_ref, w_ref, b_ref, o_ref):
    c = c_ref[...]
    c_act = (c * jax.nn.sigmoid(c)).astype(BF16)
    o_ref[...] = jnp.dot(c_act, w_ref[...].astype(BF16),
                         preferred_element_type=F32) + b_ref[...]


def _modulation(c, mod_w, mod_b):
    depth, d, n = mod_w.shape
    b = c.shape[0]
    tn = MOD_COL_TILE
    return pl.pallas_call(
        _mod_kernel,
        out_shape=jax.ShapeDtypeStruct((depth, b, n), F32),
        grid=(depth, n // tn),
        in_specs=[
            pl.BlockSpec((b, d), lambda l, j: (0, 0)),
            pl.BlockSpec((None, d, tn), lambda l, j: (l, 0, j)),
            pl.BlockSpec((None, 1, tn), lambda l, j: (l, 0, j)),
        ],
        out_specs=pl.BlockSpec((None, b, tn), lambda l, j: (l, 0, j)),
        compiler_params=_params(2),
        name="modulation",
    )(c, mod_w, mod_b.reshape(depth, 1, n))


def _proj_kernel(*refs, n_groups, col_blocks_per_group, rotary):
    x_ref, gain_ref, shift_ref, scale_ref = refs[:4]
    rest = list(refs[4:])
    perm_ref = rest.pop(0) if n_groups > 1 else None
    cos_ref, sin_ref = (rest.pop(0), rest.pop(0)) if rotary else (None, None)
    w_ref, o_ref, h_ref = rest
    j = pl.program_id(1)
    tm = x_ref.shape[0]

    @pl.when(j == 0)
    def _():
        _norm_mod_rows(x_ref, gain_ref, shift_ref, scale_ref, h_ref.at[0])
        for g in range(1, n_groups):
            for t0 in range(0, tm, PERM_TILE):
                rs = slice(t0, t0 + PERM_TILE)
                h_ref[g, rs, :] = jnp.dot(perm_ref[g - 1], h_ref[0, rs, :],
                                          preferred_element_type=F32).astype(BF16)

    g = j // col_blocks_per_group
    y = jnp.dot(h_ref[g], w_ref[...], preferred_element_type=F32)
    if rotary:
        cos, sin = cos_ref[...], sin_ref[...]
        half = cos.shape[1]
        for c0 in range(0, y.shape[1], 2 * half):
            y1, y2 = y[:, c0:c0 + half], y[:, c0 + half:c0 + 2 * half]
            o_ref[:, c0:c0 + half] = (y1 * cos - y2 * sin).astype(o_ref.dtype)
            o_ref[:, c0 + half:c0 + 2 * half] = (y1 * sin + y2 * cos).astype(o_ref.dtype)
    else:
        o_ref[...] = y.astype(o_ref.dtype)


def _project(x2, seq, gain, shift, scale, w, dilations, rotary=None):
    m, d = x2.shape
    n = w.shape[1]
    tm, tn = PROJ_ROW_TILE, PROJ_COL_TILE
    n_groups = len(dilations)
    tiles_per_seq = seq // tm
    batch = lambda i, j: (i // tiles_per_seq, 0, 0)
    in_specs = [
        pl.BlockSpec((tm, d), lambda i, j: (i, 0)),
        pl.BlockSpec((1, d), lambda i, j: (0, 0)),
        pl.BlockSpec((None, 1, d), batch),
        pl.BlockSpec((None, 1, d), batch),
    ]
    args = [x2, gain, shift, scale]
    if n_groups > 1:
        assert dilations[0] == 1
        perms = jnp.asarray(np.stack([_class_major_perm(dl) for dl in dilations[1:]]), BF16)
        in_specs.append(pl.BlockSpec(perms.shape, lambda i, j: (0, 0, 0)))
        args.append(perms)
    if rotary is not None:
        cos, sin, cols_per_kind = rotary
        blocks_per_kind = cols_per_kind // tn
        last_kind = cos.shape[0] - 1
        table = lambda i, j: (jnp.minimum(j // blocks_per_kind, last_kind),
                              i % tiles_per_seq, 0)
        in_specs += [pl.BlockSpec((None, tm, cos.shape[2]), table)] * 2
        args += [cos, sin]
    in_specs.append(pl.BlockSpec((d, tn), lambda i, j: (0, j)))
    args.append(w)
    return pl.pallas_call(
        functools.partial(_proj_kernel, n_groups=n_groups,
                          col_blocks_per_group=n // n_groups // tn,
                          rotary=rotary is not None),
        out_shape=jax.ShapeDtypeStruct((m, n), BF16),
        grid=(m // tm, n // tn),
        in_specs=in_specs,
        out_specs=pl.BlockSpec((tm, tn), lambda i, j: (i, j)),
        scratch_shapes=[pltpu.VMEM((n_groups, tm, d), BF16)],
        compiler_params=_params(2),
        name="in_project",
    )(*args)


def _row_pieces(c, r0, n):
    pieces = []
    r = r0
    while r < r0 + n:
        stop = min((r // c + 1) * c, r0 + n)
        pieces.append((r // c, r % c, stop - (r // c) * c))
        r = stop
    return pieces


def _load_rows(ref, r0, n, cs):
    parts = [ref[k, a:b, cs] for k, a, b in _row_pieces(ref.shape[1], r0, n)]
    return parts[0] if len(parts) == 1 else jnp.concatenate(parts, axis=0)


def _store_rows(ref, r0, cs, val):
    off = 0
    for k, a, b in _row_pieces(ref.shape[1], r0, val.shape[0]):
        ref[k, a:b, cs] = val[off:off + b - a]
        off += b - a


def _attn_kernel(q_ref, k_ref, v_ref, kp_ref, vp_ref, o_ref, lse_ref, *, tq, hb):
    n = pl.program_id(2)
    hblk = pl.program_id(3)
    w = ATTN_STEPS
    nq = tq // w
    scale = HEAD_DIM_A ** -0.5
    all_lanes = slice(0, LANES)

    @pl.when(hblk == 0)
    def _():
        lse_ref[...] = jnp.zeros_like(lse_ref)

    qi = lax.broadcasted_iota(jnp.int32, (w, 2 * w), 0)
    kj = lax.broadcasted_iota(jnp.int32, (w, 2 * w), 1)
    band = (kj >= qi) & (kj <= qi + w)
    band_first = band & ((kj >= w) | (n > 0))
    lane = lax.broadcasted_iota(jnp.int32, (w, LANES), 1)

    for hh in range(hb):
        cs = slice(hh * HEAD_DIM_A, (hh + 1) * HEAD_DIM_A)
        head = hblk * hb + hh
        for qb in range(nq):
            q = _load_rows(q_ref, qb * w, w, cs)
            if qb == 0:
                kcat = jnp.concatenate([_load_rows(kp_ref, 0, w, cs),
                                        _load_rows(k_ref, 0, w, cs)], axis=0)
                vcat = jnp.concatenate([_load_rows(vp_ref, 0, w, cs),
                                        _load_rows(v_ref, 0, w, cs)], axis=0)
                mask = band_first
            else:
                kcat = _load_rows(k_ref, (qb - 1) * w, 2 * w, cs)
                vcat = _load_rows(v_ref, (qb - 1) * w, 2 * w, cs)
                mask = band
            s = lax.dot_general(q, kcat, (((1,), (1,)), ((), ())),
                                preferred_element_type=F32)
            s = jnp.where(mask, s, -jnp.inf)
            m = jnp.max(s, axis=-1, keepdims=True)
            p = jnp.exp2((s - m) * (scale * LOG2_E))
            den = jnp.sum(p, axis=-1, keepdims=True)
            o = jnp.dot(p.astype(BF16), vcat, preferred_element_type=F32) / den
            _store_rows(o_ref, qb * w, cs, o.astype(o_ref.dtype))
            lse = m * scale + jnp.log(den)
            prev = _load_rows(lse_ref, qb * w, w, all_lanes)
            _store_rows(lse_ref, qb * w, all_lanes, jnp.where(lane == head, lse, prev))


def _dilated_attention(qkv, seq, group, dilation):
    m, n_all = qkv.shape
    b = m // seq
    d = dilation
    c = PERM_TILE // d
    tiles = seq // PERM_TILE
    tq = min(ATTN_ROW_TILE, seq // d)
    tpq = tq // c
    hb = ATTN_BLOCKS_PER_STEP // (tq // ATTN_STEPS)
    cw = hb * HEAD_DIM_A
    ncb = WIDTH_A // cw
    w = ATTN_STEPS
    col0 = group * 3 * ncb
    pc = min(c, w)
    ptiles = w // pc
    rows_view = lambda a: a.reshape(b, tiles, d, c, a.shape[-1])

    def prev_map(section):
        if c >= w:
            return lambda bi, r, nn, h: (bi, jnp.maximum(nn * tpq - 1, 0), r, c // w - 1,
                                         col0 + section * ncb + h)
        return lambda bi, r, nn, h: (bi, jnp.maximum(nn * (tpq // ptiles) - 1, 0), r, 0,
                                     col0 + section * ncb + h)

    def cur_map(section):
        return lambda bi, r, nn, h: (bi, nn, r, 0, col0 + section * ncb + h)

    qv = rows_view(qkv)
    o, lse = pl.pallas_call(
        functools.partial(_attn_kernel, tq=tq, hb=hb),
        out_shape=(jax.ShapeDtypeStruct((b, tiles, d, c, WIDTH_A), BF16),
                   jax.ShapeDtypeStruct((b, tiles, d, c, LANES), F32)),
        grid=(b, d, seq // d // tq, ncb),
        in_specs=[
            pl.BlockSpec((None, tpq, None, c, cw), cur_map(0)),
            pl.BlockSpec((None, tpq, None, c, cw), cur_map(1)),
            pl.BlockSpec((None, tpq, None, c, cw), cur_map(2)),
            pl.BlockSpec((None, ptiles, None, pc, cw), prev_map(1)),
            pl.BlockSpec((None, ptiles, None, pc, cw), prev_map(2)),
        ],
        out_specs=(
            pl.BlockSpec((None, tpq, None, c, cw), lambda bi, r, nn, h: (bi, nn, r, 0, h)),
            pl.BlockSpec((None, tpq, None, c, LANES), lambda bi, r, nn, h: (bi, nn, r, 0, 0)),
        ),
        compiler_params=_params(4),
        name=f"dilated_attention_d{d}",
    )(qv, qv, qv, qv, qv)
    return o.reshape(m, WIDTH_A), lse.reshape(m, LANES)


def _merge_tile(o_refs, lse_refs, pt_ref, lsn_refs, a_ref, dilations):
    ls = [lse_refs[0][...]]
    for g in range(1, len(dilations)):
        d = dilations[g]
        c = PERM_TILE // d
        for r in range(d):
            lsn_refs[g - 1][pl.ds(r, c, stride=d), :] = lse_refs[g][r * c:(r + 1) * c, :]
        ls.append(lsn_refs[g - 1][...])
    mx = functools.reduce(jnp.maximum, ls)
    es = [jnp.exp(l - mx) for l in ls]
    inv = 1.0 / functools.reduce(lambda u, v: u + v, es)
    alphas = [e * inv for e in es[1:]]
    heads_per_dot = 2
    for hp in range(N_HEADS_A // heads_per_dot):
        cs2 = slice(hp * heads_per_dot * HEAD_DIM_A, (hp + 1) * heads_per_dot * HEAD_DIM_A)
        base = o_refs[0][:, cs2].astype(F32)
        others = [jnp.dot(pt_ref[g - 1], o_refs[g][:, cs2], preferred_element_type=F32)
                  for g in range(1, len(dilations))]
        for hh in range(heads_per_dot):
            h = hp * heads_per_dot + hh
            ls_ = slice(hh * HEAD_DIM_A, (hh + 1) * HEAD_DIM_A)
            acc = base[:, ls_]
            for al, og in zip(alphas, others):
                acc = acc + al[:, h:h + 1] * (og[:, ls_] - base[:, ls_])
            a_ref[:, h * HEAD_DIM_A:(h + 1) * HEAD_DIM_A] = acc.astype(a_ref.dtype)


def _attn_out_kernel(o0, o1, o2, l0, l1, l2, pt_ref, w_ref, x_ref, gate_ref, out_ref,
                     a0_ref, a1_ref, lsn1_ref, lsn2_ref, *, dilations):
    s = pl.program_id(0)

    @pl.when(s == 0)
    def _():
        a1_ref[...] = jnp.zeros_like(a1_ref)

    def step(a_new, a_old):
        _merge_tile((o0, o1, o2), (l0, l1, l2), pt_ref, (lsn1_ref, lsn2_ref), a_new, dilations)
        y = jnp.dot(a_old[...], w_ref[...], preferred_element_type=F32)
        out_ref[...] = x_ref[...] + gate_ref[...] * y

    @pl.when(s % 2 == 0)
    def _():
        step(a0_ref, a1_ref)

    @pl.when(s % 2 == 1)
    def _():
        step(a1_ref, a0_ref)


def _attn_out_project(x2, seq, gate, w, outs, lses, dilations):
    m, d = x2.shape
    k = w.shape[0]
    tm = PERM_TILE
    n_tiles = m // tm
    tiles_per_seq = seq // tm
    pts = jnp.asarray(np.stack([_class_major_perm(dl).T for dl in dilations[1:]]), BF16)
    cur = lambda s: (jnp.minimum(s, n_tiles - 1), 0)
    prev = lambda s: (jnp.maximum(s - 1, 0), 0)
    in_specs = ([pl.BlockSpec((tm, k), cur) for _ in outs]
                + [pl.BlockSpec((tm, LANES), cur) for _ in lses]
                + [pl.BlockSpec(pts.shape, lambda s: (0, 0, 0)),
                   pl.BlockSpec((k, d), lambda s: (0, 0), pipeline_mode=pl.Buffered(1)),
                   pl.BlockSpec((tm, d), prev),
                   pl.BlockSpec((None, 1, d),
                                lambda s: (jnp.maximum(s - 1, 0) // tiles_per_seq, 0, 0))])
    return pl.pallas_call(
        functools.partial(_attn_out_kernel, dilations=dilations),
        out_shape=jax.ShapeDtypeStruct((m, d), F32),
        grid=(n_tiles + 1,),
        in_specs=in_specs,
        out_specs=pl.BlockSpec((tm, d), prev),
        scratch_shapes=[pltpu.VMEM((tm, k), BF16), pltpu.VMEM((tm, k), BF16),
                        pltpu.VMEM((tm, LANES), F32), pltpu.VMEM((tm, LANES), F32)],
        compiler_params=_params(1),
        name="attn_out_project",
    )(*outs, *lses, pts, w, x2, gate)


def _out_kernel(a_ref, w_ref, x_ref, gate_ref, out_ref):
    y = jnp.dot(a_ref[...], w_ref[...], preferred_element_type=F32)
    out_ref[...] = x_ref[...] + gate_ref[...] * y


def _out_project(x2, seq, gate, w, a):
    m, d = x2.shape
    k = w.shape[0]
    tm, tn = OUT_ROW_TILE, OUT_COL_TILE
    tiles_per_seq = seq // tm
    return pl.pallas_call(
        _out_kernel,
        out_shape=jax.ShapeDtypeStruct((m, d), F32),
        grid=(m // tm, d // tn),
        in_specs=[
            pl.BlockSpec((tm, k), lambda i, j: (i, 0)),
            pl.BlockSpec((k, tn), lambda i, j: (0, j)),
            pl.BlockSpec((tm, tn), lambda i, j: (i, j)),
            pl.BlockSpec((None, 1, tn), lambda i, j: (i // tiles_per_seq, 0, j)),
        ],
        out_specs=pl.BlockSpec((tm, tn), lambda i, j: (i, j)),
        compiler_params=_params(2),
        name="out_project",
    )(a, w, x2, gate)


def _retention_kernel(q_ref, k_ref, v_ref, g_ref, decay_ref, xi_ref, zeta_ref, gamc_ref,
                      gn_ref, o_ref, r_ref, *, tc):
    c = RET_CHUNK

    @pl.when(pl.program_id(2) == 0)
    def _():
        r_ref[...] = jnp.zeros_like(r_ref)

    decay = decay_ref[...]
    xi = xi_ref[...]
    zeta = zeta_ref[...]
    gamc = gamc_ref[...]
    gn = gn_ref[...]

    for ci in range(tc // c):
        rs = slice(ci * c, (ci + 1) * c)
        q = q_ref[rs, :]
        k = k_ref[rs, :]
        v = v_ref[rs, :]
        s = lax.dot_general(q, k, (((1,), (1,)), ((), ())), preferred_element_type=F32) * decay
        inner = jnp.dot(s.astype(BF16), v, preferred_element_type=F32)
        r = r_ref[...]
        cross = jnp.dot(q, r.astype(BF16), preferred_element_type=F32) * xi
        kz = (k.astype(F32) * zeta).astype(BF16)
        kv = lax.dot_general(kz, v, (((0,), (0,)), ((), ())), preferred_element_type=F32)
        r_ref[...] = gamc * r + kv
        y = inner + cross
        mu = jnp.mean(y, axis=-1, keepdims=True)
        yc = y - mu
        var = jnp.mean(yc * yc, axis=-1, keepdims=True)
        yn = yc * lax.rsqrt(var + EPS) * gn
        g = g_ref[rs, :].astype(F32)
        o_ref[rs, :] = (_silu(g) * yn).astype(o_ref.dtype)


def _rotary_tables(seq, dk):
    half = dk // 2
    pos = jnp.arange(seq, dtype=F32)
    freqs = ROPE_BASE ** (-jnp.arange(half, dtype=F32) / half)
    ang = pos[:, None] * freqs[None, :]
    cos, sin = jnp.cos(ang), jnp.sin(ang)
    kscale = dk ** -0.5
    return (jnp.stack([cos, cos * kscale, jnp.ones_like(cos)]),
            jnp.stack([sin, sin * kscale, jnp.zeros_like(sin)]))


def _retention_tables():
    h, c = N_HEADS_B, RET_CHUNK
    log_gamma = jnp.log1p(-jnp.exp2(-5.0 - jnp.arange(h, dtype=F32)))
    idx = jnp.arange(c, dtype=F32)
    rel = idx[:, None] - idx[None, :]
    decay = jnp.where(rel >= 0, jnp.exp(log_gamma[:, None, None] * jnp.maximum(rel, 0.0)), 0.0)
    xi = jnp.exp(log_gamma[:, None] * (idx + 1.0))[:, :, None]
    zeta = jnp.exp(log_gamma[:, None] * (c - 1.0 - idx))[:, :, None]
    gamc = jnp.exp(log_gamma * c)[:, None, None]
    return decay, xi, zeta, gamc


def _retention(proj, gn_g):
    b, s, n = proj.shape
    h = N_HEADS_B
    dv = gn_g.shape[-1] // h
    dk = (n - 2 * h * dv) // (2 * h)
    tc = RET_ROW_TILE
    c = RET_CHUNK
    decay, xi, zeta, gamc = _retention_tables()
    v_blk0 = (2 * h * dk) // dv
    out = pl.pallas_call(
        functools.partial(_retention_kernel, tc=tc),
        out_shape=jax.ShapeDtypeStruct((b, s, h * dv), BF16),
        grid=(b, h, s // tc),
        in_specs=[
            pl.BlockSpec((None, tc, dk), lambda bi, hi, t: (bi, t, hi)),
            pl.BlockSpec((None, tc, dk), lambda bi, hi, t: (bi, t, h + hi)),
            pl.BlockSpec((None, tc, dv), lambda bi, hi, t: (bi, t, v_blk0 + hi)),
            pl.BlockSpec((None, tc, dv), lambda bi, hi, t: (bi, t, v_blk0 + h + hi)),
            pl.BlockSpec((None, c, c), lambda bi, hi, t: (hi, 0, 0)),
            pl.BlockSpec((None, c, 1), lambda bi, hi, t: (hi, 0, 0)),
            pl.BlockSpec((None, c, 1), lambda bi, hi, t: (hi, 0, 0)),
            pl.BlockSpec((None, 1, 1), lambda bi, hi, t: (hi, 0, 0)),
            pl.BlockSpec((1, dv), lambda bi, hi, t: (0, hi)),
        ],
        out_specs=pl.BlockSpec((None, tc, dv), lambda bi, hi, t: (bi, t, hi)),
        scratch_shapes=[pltpu.VMEM((dk, dv), F32)],
        compiler_params=_params(3),
        name="retention",
    )(proj, proj, proj, proj, decay, xi, zeta, gamc, gn_g.reshape(1, h * dv))
    return out.reshape(b * s, h * dv)


def _ffn_kernel(x_ref, gain_ref, shift_ref, scale_ref, gate_ref, wa_ref, wb_ref,
                cwa_ref, cwb_ref, cba_ref, cbb_ref, wd_ref, fg_ref, o_ref, h_ref,
                ua_ref, ub_ref, carry_ref, *, tiles_per_seq, final_norm):
    i = pl.program_id(0)
    f = pl.program_id(1)
    pad = CONV_HALO_ROWS
    tm = x_ref.shape[0]
    u_refs = (ua_ref, ub_ref)

    @pl.when(f == 0)
    def _():
        _norm_mod_rows(x_ref, gain_ref, shift_ref, scale_ref, h_ref)
        o_ref[...] = jnp.zeros_like(o_ref)

    @pl.when(i % tiles_per_seq == 0)
    def _():
        for u_ref in u_refs:
            u_ref[0:pad, :] = jnp.zeros((pad, u_ref.shape[1]), F32)

    @pl.when(i % tiles_per_seq != 0)
    def _():
        for br, u_ref in enumerate(u_refs):
            u_ref[0:pad, :] = carry_ref[f, br]

    def up_project(r0, r1):
        hr = h_ref[r0:r1, :]
        for u_ref, w_ref in zip(u_refs, (wa_ref, wb_ref)):
            u_ref[pad + r0:pad + r1, :] = jnp.dot(hr, w_ref[...], preferred_element_type=F32)

    def conv(u_ref, cw_ref, cb_ref, r0, r1):
        cw = cw_ref[...]
        y = cw[2:3, :] * u_ref[pad + r0:pad + r1, :]
        y = y + cw[1:2, :] * u_ref[pad - 1 + r0:pad - 1 + r1, :]
        y = y + cw[0:1, :] * u_ref[pad - 2 + r0:pad - 2 + r1, :]
        return y + cb_ref[...]

    def down_project(r0, r1):
        a = conv(ua_ref, cwa_ref, cba_ref, r0, r1)
        b = conv(ub_ref, cwb_ref, cbb_ref, r0, r1)
        act = (_silu(a) * b).astype(BF16)
        o_ref[r0:r1, :] += jnp.dot(act, wd_ref[...], preferred_element_type=F32)

    bounds = list(range(0, tm + 1, FFN_ROW_CHUNK))
    chunks = list(zip(bounds[:-1], bounds[1:]))
    up_project(*chunks[0])
    for ci, (r0, r1) in enumerate(chunks):
        if ci + 1 < len(chunks):
            up_project(*chunks[ci + 1])
        down_project(r0, r1)

    for br, u_ref in enumerate(u_refs):
        carry_ref[f, br] = u_ref[tm:tm + pad, :]

    @pl.when(f == pl.num_programs(1) - 1)
    def _():
        xn = x_ref[...] + gate_ref[...] * o_ref[...]
        if final_norm:
            xn = _rmsnorm(xn, fg_ref[...])
        o_ref[...] = xn


def _conv_ffn(x2, seq, gain, shift, scale, gate, w_up, conv_w, conv_b, w_down, final_gain,
              layer, final_norm):
    m, d = x2.shape
    ff = w_down.shape[1]
    tm, tf = FFN_ROW_TILE, FFN_COL_TILE
    nf = ff // tf
    tiles_per_seq = seq // tm
    batch = lambda i, f: (i // tiles_per_seq, 0, 0)
    const = lambda i, f: (0, 0)
    return pl.pallas_call(
        functools.partial(_ffn_kernel, tiles_per_seq=tiles_per_seq, final_norm=final_norm),
        out_shape=jax.ShapeDtypeStruct((m, d), F32),
        grid=(m // tm, nf),
        in_specs=[
            pl.BlockSpec((tm, d), lambda i, f: (i, 0), pipeline_mode=pl.Buffered(1)),
            pl.BlockSpec((1, d), const),
            pl.BlockSpec((None, 1, d), batch),
            pl.BlockSpec((None, 1, d), batch),
            pl.BlockSpec((None, 1, d), batch),
            pl.BlockSpec((None, d, tf), lambda i, f: (layer, 0, f)),
            pl.BlockSpec((None, d, tf), lambda i, f: (layer, 0, nf + f)),
            pl.BlockSpec((None, CONV_WIDTH, tf), lambda i, f: (layer, 0, f)),
            pl.BlockSpec((None, CONV_WIDTH, tf), lambda i, f: (layer, 0, nf + f)),
            pl.BlockSpec((None, 1, tf), lambda i, f: (layer, 0, f)),
            pl.BlockSpec((None, 1, tf), lambda i, f: (layer, 0, nf + f)),
            pl.BlockSpec((None, tf, d), lambda i, f: (layer, f, 0)),
            pl.BlockSpec((1, d), const),
        ],
        out_specs=pl.BlockSpec((tm, d), lambda i, f: (i, 0)),
        scratch_shapes=[pltpu.VMEM((tm, d), BF16),
                        pltpu.VMEM((tm + CONV_HALO_ROWS, tf), F32),
                        pltpu.VMEM((tm + CONV_HALO_ROWS, tf), F32),
                        pltpu.VMEM((nf, 2, CONV_HALO_ROWS, tf), F32)],
        compiler_params=_params(2),
        name="conv_ffn",
    )(x2, gain, shift, scale, gate, w_up, w_up, conv_w, conv_w, conv_b, conv_b, w_down,
      final_gain)


def kernel(x, c, mod_w, mod_b, norm_mix, w_in_a, w_out_a, w_in_b, gn_b, w_out_b, norm_ffn,
           ffn_up, ffn_conv_w, ffn_conv_b, ffn_down, final_norm):
    b, s, d = x.shape
    depth = mod_w.shape[0]
    mod = _modulation(c, mod_w, mod_b).reshape(depth, b, 6, 1, d)
    x2 = x.reshape(b * s, d)
    dilations = tuple(dl for _, dl in DILATED_GROUPS)
    ffn_up_bf = ffn_up.astype(BF16)
    ffn_down_bf = ffn_down.astype(BF16)
    conv_w = ffn_conv_w.reshape(depth, CONV_WIDTH, -1)
    conv_b = ffn_conv_b.reshape(depth, 1, -1)

    for i in range(depth):
        sh_a, sc_a, g_a, sh_m, sc_m, g_m = (mod[i, :, k] for k in range(6))
        gain = norm_mix[i].reshape(1, d)
        j = i // 2
        if i % 2 == 0:
            qkv = _project(x2, s, gain, sh_a, sc_a, w_in_a[j].astype(BF16), dilations)
            outs, lses = [], []
            for g, dil in enumerate(dilations):
                o, lse = _dilated_attention(qkv, s, g, dil)
                outs.append(o)
                lses.append(lse)
            x2 = _attn_out_project(x2, s, g_a, w_out_a[j].astype(BF16), outs, lses, dilations)
        else:
            dv = gn_b.shape[-1] // N_HEADS_B
            qk_cols = (w_in_b.shape[-1] - 2 * N_HEADS_B * dv) // 2
            cos, sin = _rotary_tables(s, qk_cols // N_HEADS_B)
            proj = _project(x2, s, gain, sh_a, sc_a, w_in_b[j].astype(BF16), (1,),
                            rotary=(cos, sin, qk_cols))
            a = _retention(proj.reshape(b, s, -1), gn_b[j])
            x2 = _out_project(x2, s, g_a, w_out_b[j].astype(BF16), a)
        x2 = _conv_ffn(
            x2, s, norm_ffn[i].reshape(1, d), sh_m, sc_m, g_m, ffn_up_bf, conv_w, conv_b,
            ffn_down_bf, final_norm.reshape(1, d), layer=i, final_norm=(i == depth - 1))
    return x2.reshape(b, s, d)
```

```python
import functools

import jax
import jax.numpy as jnp
import numpy as np
from jax import lax
from jax.experimental import pallas as pl
from jax.experimental.pallas import tpu as pltpu

F32 = jnp.float32
BF16 = jnp.bfloat16

EPS = 1e-6
LOG2_E = 1.4426950408889634
HEAD_DIM_A = 128
N_HEADS_A = 16
WIDTH_A = N_HEADS_A * HEAD_DIM_A
DILATED_GROUPS = ((128, 1), (512, 4), (2048, 16))
ATTN_STEPS = 128
N_HEADS_B = 8
RET_CHUNK = 256
ROPE_BASE = 10000.0
CONV_WIDTH = 3

VMEM_LIMIT_BYTES = 60 * 1024 * 1024
LANES = 128
CONV_HALO_ROWS = 8

PERM_TILE = 256
ATTN_OUT_ROW_TILE = 512
PROJ_ROW_TILE = 1024
PROJ_COL_TILES = (1024, 2048)
PROJ_VMEM_BUDGET_BYTES = 54 * 1024 * 1024
NORM_ROW_CHUNK = 16
NORM_UNROLL = 16
OUT_ROW_TILE = 1024
OUT_COL_TILE = 512
FFN_ROW_TILE = 1024
FFN_COL_TILE = 512
FFN_ROW_CHUNK = 512
ATTN_ROW_TILE = 1024
ATTN_BLOCKS_PER_STEP = 32
RET_ROW_TILE = 1024
MOD_COL_TILE = 1024


def _params(n_axes):
    return pltpu.CompilerParams(
        dimension_semantics=("arbitrary",) * n_axes,
        vmem_limit_bytes=VMEM_LIMIT_BYTES)


def _silu(x):
    h = 0.5 * x
    return h + h * jnp.tanh(h)


def _rmsnorm(x, gain):
    ms = jnp.mean(x * x, axis=-1, keepdims=True)
    return x * lax.rsqrt(ms + EPS) * gain


def _norm_mod_rows(x_ref, gain_ref, shift_ref, scale_ref, h_ref):
    shift = shift_ref[...]
    gain_mod = gain_ref[...] * (1.0 + scale_ref[...])
    rows = NORM_ROW_CHUNK

    def body(k, carry):
        rs = pl.ds(pl.multiple_of(k * rows, rows), rows)
        h_ref[rs, :] = (_rmsnorm(x_ref[rs, :], gain_mod) + shift).astype(h_ref.dtype)
        return carry

    lax.fori_loop(0, x_ref.shape[0] // rows, body, 0, unroll=NORM_UNROLL)


def _class_major_perm(dilation):
    c = PERM_TILE // dilation
    p = np.arange(PERM_TILE)
    m = np.zeros((PERM_TILE, PERM_TILE), np.float32)
    m[p, (p % c) * dilation + p // c] = 1.0
    return m


def _mod_kernel(c_ref, w_ref, b_ref, o_ref):
    c = c_ref[...]
    c_act = (c * jax.nn.sigmoid(c)).astype(BF16)
    o_ref[...] = jnp.dot(c_act, w_ref[...].astype(BF16),
                         preferred_element_type=F32) + b_ref[...]


def _modulation(c, mod_w, mod_b):
    depth, d, n = mod_w.shape
    b = c.shape[0]
    tn = MOD_COL_TILE
    return pl.pallas_call(
        _mod_kernel,
        out_shape=jax.ShapeDtypeStruct((depth, b, n), F32),
        grid=(depth, n // tn),
        in_specs=[
            pl.BlockSpec((b, d), lambda l, j: (0, 0)),
            pl.BlockSpec((None, d, tn), lambda l, j: (l, 0, j)),
            pl.BlockSpec((None, 1, tn), lambda l, j: (l, 0, j)),
        ],
        out_specs=pl.BlockSpec((None, b, tn), lambda l, j: (l, 0, j)),
        compiler_params=_params(2),
        name="modulation",
    )(c, mod_w, mod_b.reshape(depth, 1, n))


def _proj_kernel(*refs, n_groups, col_blocks_per_group, rotary):
    x_ref, gain_ref, shift_ref, scale_ref = refs[:4]
    rest = list(refs[4:])
    perm_ref = rest.pop(0) if n_groups > 1 else None
    cos_ref, sin_ref = (rest.pop(0), rest.pop(0)) if rotary else (None, None)
    w_ref, o_ref, h_ref = rest
    j = pl.program_id(1)
    tm = x_ref.shape[0]

    @pl.when(j == 0)
    def _():
        _norm_mod_rows(x_ref, gain_ref, shift_ref, scale_ref, h_ref.at[0])
        for g in range(1, n_groups):
            for t0 in range(0, tm, PERM_TILE):
                rs = slice(t0, t0 + PERM_TILE)
                h_ref[g, rs, :] = jnp.dot(perm_ref[g - 1], h_ref[0, rs, :],
                                          preferred_element_type=F32).astype(BF16)

    g = j // col_blocks_per_group
    y = jnp.dot(h_ref[g], w_ref[...], preferred_element_type=F32)
    if rotary:
        cos, sin = cos_ref[...], sin_ref[...]
        half = cos.shape[1]
        for c0 in range(0, y.shape[1], 2 * half):
            y1, y2 = y[:, c0:c0 + half], y[:, c0 + half:c0 + 2 * half]
            o_ref[:, c0:c0 + half] = (y1 * cos - y2 * sin).astype(o_ref.dtype)
            o_ref[:, c0 + half:c0 + 2 * half] = (y1 * sin + y2 * cos).astype(o_ref.dtype)
    else:
        o_ref[...] = y.astype(o_ref.dtype)


def _project(x2, seq, gain, shift, scale, w, dilations, rotary=None):
    m, d = x2.shape
    n = w.shape[1]
    tm = PROJ_ROW_TILE
    n_groups = len(dilations)

    def vmem_bytes(tn):
        return 2 * (tm * d * 4 + d * tn * 2 + tm * tn * 2) + n_groups * tm * d * 2 + tm * tn * 4

    tn = max(t for t in PROJ_COL_TILES if vmem_bytes(t) <= PROJ_VMEM_BUDGET_BYTES)
    tiles_per_seq = seq // tm
    batch = lambda i, j: (i // tiles_per_seq, 0, 0)
    in_specs = [
        pl.BlockSpec((tm, d), lambda i, j: (i, 0)),
        pl.BlockSpec((1, d), lambda i, j: (0, 0)),
        pl.BlockSpec((None, 1, d), batch),
        pl.BlockSpec((None, 1, d), batch),
    ]
    args = [x2, gain, shift, scale]
    if n_groups > 1:
        assert dilations[0] == 1
        perms = jnp.asarray(np.stack([_class_major_perm(dl) for dl in dilations[1:]]), BF16)
        in_specs.append(pl.BlockSpec(perms.shape, lambda i, j: (0, 0, 0)))
        args.append(perms)
    if rotary is not None:
        cos, sin, cols_per_kind = rotary
        blocks_per_kind = cols_per_kind // tn
        last_kind = cos.shape[0] - 1
        table = lambda i, j: (jnp.minimum(j // blocks_per_kind, last_kind),
                              i % tiles_per_seq, 0)
        in_specs += [pl.BlockSpec((None, tm, cos.shape[2]), table)] * 2
        args += [cos, sin]
    in_specs.append(pl.BlockSpec((d, tn), lambda i, j: (0, j)))
    args.append(w)
    return pl.pallas_call(
        functools.partial(_proj_kernel, n_groups=n_groups,
                          col_blocks_per_group=n // n_groups // tn,
                          rotary=rotary is not None),
        out_shape=jax.ShapeDtypeStruct((m, n), BF16),
        grid=(m // tm, n // tn),
        in_specs=in_specs,
        out_specs=pl.BlockSpec((tm, tn), lambda i, j: (i, j)),
        scratch_shapes=[pltpu.VMEM((n_groups, tm, d), BF16)],
        compiler_params=_params(2),
        name="in_project",
    )(*args)


def _row_pieces(c, r0, n):
    pieces = []
    r = r0
    while r < r0 + n:
        stop = min((r // c + 1) * c, r0 + n)
        pieces.append((r // c, r % c, stop - (r // c) * c))
        r = stop
    return pieces


def _load_rows(ref, r0, n, cs):
    parts = [ref[k, a:b, cs] for k, a, b in _row_pieces(ref.shape[1], r0, n)]
    return parts[0] if len(parts) == 1 else jnp.concatenate(parts, axis=0)


def _store_rows(ref, r0, cs, val):
    off = 0
    for k, a, b in _row_pieces(ref.shape[1], r0, val.shape[0]):
        ref[k, a:b, cs] = val[off:off + b - a]
        off += b - a


def _attn_kernel(q_ref, k_ref, v_ref, kp_ref, vp_ref, o_ref, lse_ref, *, tq, hb):
    n = pl.program_id(2)
    hblk = pl.program_id(3)
    w = ATTN_STEPS
    nq = tq // w
    scale = HEAD_DIM_A ** -0.5
    all_lanes = slice(0, LANES)

    @pl.when(hblk == 0)
    def _():
        lse_ref[...] = jnp.zeros_like(lse_ref)

    qi = lax.broadcasted_iota(jnp.int32, (w, 2 * w), 0)
    kj = lax.broadcasted_iota(jnp.int32, (w, 2 * w), 1)
    band = (kj >= qi) & (kj <= qi + w)
    band_first = band & ((kj >= w) | (n > 0))
    lane = lax.broadcasted_iota(jnp.int32, (w, LANES), 1)

    for hh in range(hb):
        cs = slice(hh * HEAD_DIM_A, (hh + 1) * HEAD_DIM_A)
        head = hblk * hb + hh
        for qb in range(nq):
            q = _load_rows(q_ref, qb * w, w, cs)
            if qb == 0:
                kcat = jnp.concatenate([_load_rows(kp_ref, 0, w, cs),
                                        _load_rows(k_ref, 0, w, cs)], axis=0)
                vcat = jnp.concatenate([_load_rows(vp_ref, 0, w, cs),
                                        _load_rows(v_ref, 0, w, cs)], axis=0)
                mask = band_first
            else:
                kcat = _load_rows(k_ref, (qb - 1) * w, 2 * w, cs)
                vcat = _load_rows(v_ref, (qb - 1) * w, 2 * w, cs)
                mask = band
            s = lax.dot_general(q, kcat, (((1,), (1,)), ((), ())),
                                preferred_element_type=F32)
            s = jnp.where(mask, s, -jnp.inf)
            m = jnp.max(s, axis=-1, keepdims=True)
            p = jnp.exp2((s - m) * (scale * LOG2_E))
            den = jnp.sum(p, axis=-1, keepdims=True)
            o = jnp.dot(p.astype(BF16), vcat, preferred_element_type=F32) / den
            _store_rows(o_ref, qb * w, cs, o.astype(o_ref.dtype))
            lse = m * scale + jnp.log(den)
            prev = _load_rows(lse_ref, qb * w, w, all_lanes)
            _store_rows(lse_ref, qb * w, all_lanes, jnp.where(lane == head, lse, prev))


def _dilated_attention(qkv, seq, group, dilation):
    m, n_all = qkv.shape
    b = m // seq
    d = dilation
    c = PERM_TILE // d
    tiles = seq // PERM_TILE
    tq = min(ATTN_ROW_TILE, seq // d)
    tpq = tq // c
    hb = ATTN_BLOCKS_PER_STEP // (tq // ATTN_STEPS)
    cw = hb * HEAD_DIM_A
    ncb = WIDTH_A // cw
    w = ATTN_STEPS
    col0 = group * 3 * ncb
    pc = min(c, w)
    ptiles = w // pc
    rows_view = lambda a: a.reshape(b, tiles, d, c, a.shape[-1])

    def prev_map(section):
        if c >= w:
            return lambda bi, r, nn, h: (bi, jnp.maximum(nn * tpq - 1, 0), r, c // w - 1,
                                         col0 + section * ncb + h)
        return lambda bi, r, nn, h: (bi, jnp.maximum(nn * (tpq // ptiles) - 1, 0), r, 0,
                                     col0 + section * ncb + h)

    def cur_map(section):
        return lambda bi, r, nn, h: (bi, nn, r, 0, col0 + section * ncb + h)

    qv = rows_view(qkv)
    o, lse = pl.pallas_call(
        functools.partial(_attn_kernel, tq=tq, hb=hb),
        out_shape=(jax.ShapeDtypeStruct((b, tiles, d, c, WIDTH_A), BF16),
                   jax.ShapeDtypeStruct((b, tiles, d, c, LANES), F32)),
        grid=(b, d, seq // d // tq, ncb),
        in_specs=[
            pl.BlockSpec((None, tpq, None, c, cw), cur_map(0)),
            pl.BlockSpec((None, tpq, None, c, cw), cur_map(1)),
            pl.BlockSpec((None, tpq, None, c, cw), cur_map(2)),
            pl.BlockSpec((None, ptiles, None, pc, cw), prev_map(1)),
            pl.BlockSpec((None, ptiles, None, pc, cw), prev_map(2)),
        ],
        out_specs=(
            pl.BlockSpec((None, tpq, None, c, cw), lambda bi, r, nn, h: (bi, nn, r, 0, h)),
            pl.BlockSpec((None, tpq, None, c, LANES), lambda bi, r, nn, h: (bi, nn, r, 0, 0)),
        ),
        compiler_params=_params(4),
        name=f"dilated_attention_d{d}",
    )(qv, qv, qv, qv, qv)
    return o.reshape(m, WIDTH_A), lse.reshape(m, LANES)


def _merge_tile(o_refs, lse_refs, pt_ref, lsn_refs, a_ref, dilations):
    perm_tiles = range(0, a_ref.shape[0], PERM_TILE)
    ls = [lse_refs[0][...]]
    for g in range(1, len(dilations)):
        d = dilations[g]
        c = PERM_TILE // d
        for t0 in perm_tiles:
            for r in range(d):
                lsn_refs[g - 1][pl.ds(t0 + r, c, stride=d), :] = (
                    lse_refs[g][t0 + r * c:t0 + (r + 1) * c, :])
        ls.append(lsn_refs[g - 1][...])
    mx = functools.reduce(jnp.maximum, ls)
    es = [jnp.exp(l - mx) for l in ls]
    inv = 1.0 / functools.reduce(lambda u, v: u + v, es)
    alphas = [e * inv for e in es[1:]]
    heads_per_dot = 2
    for hp in range(N_HEADS_A // heads_per_dot):
        cs2 = slice(hp * heads_per_dot * HEAD_DIM_A, (hp + 1) * heads_per_dot * HEAD_DIM_A)
        base = o_refs[0][:, cs2].astype(F32)
        others = [jnp.concatenate(
            [jnp.dot(pt_ref[g - 1], o_refs[g][t0:t0 + PERM_TILE, cs2],
                     preferred_element_type=F32) for t0 in perm_tiles], axis=0)
            for g in range(1, len(dilations))]
        for hh in range(heads_per_dot):
            h = hp * heads_per_dot + hh
            ls_ = slice(hh * HEAD_DIM_A, (hh + 1) * HEAD_DIM_A)
            acc = base[:, ls_]
            for al, og in zip(alphas, others):
                acc = acc + al[:, h:h + 1] * (og[:, ls_] - base[:, ls_])
            a_ref[:, h * HEAD_DIM_A:(h + 1) * HEAD_DIM_A] = acc.astype(a_ref.dtype)


def _attn_out_kernel(o0, o1, o2, l0, l1, l2, pt_ref, w_ref, x_ref, gate_ref, out_ref,
                     a0_ref, a1_ref, lsn1_ref, lsn2_ref, *, dilations):
    s = pl.program_id(0)

    @pl.when(s == 0)
    def _():
        a1_ref[...] = jnp.zeros_like(a1_ref)

    def step(a_new, a_old):
        _merge_tile((o0, o1, o2), (l0, l1, l2), pt_ref, (lsn1_ref, lsn2_ref), a_new, dilations)
        y = jnp.dot(a_old[...], w_ref[...], preferred_element_type=F32)
        out_ref[...] = x_ref[...] + gate_ref[...] * y

    @pl.when(s % 2 == 0)
    def _():
        step(a0_ref, a1_ref)

    @pl.when(s % 2 == 1)
    def _():
        step(a1_ref, a0_ref)


def _attn_out_project(x2, seq, gate, w, outs, lses, dilations):
    m, d = x2.shape
    k = w.shape[0]
    tm = ATTN_OUT_ROW_TILE
    n_tiles = m // tm
    tiles_per_seq = seq // tm
    pts = jnp.asarray(np.stack([_class_major_perm(dl).T for dl in dilations[1:]]), BF16)
    cur = lambda s: (jnp.minimum(s, n_tiles - 1), 0)
    prev = lambda s: (jnp.maximum(s - 1, 0), 0)
    in_specs = ([pl.BlockSpec((tm, k), cur) for _ in outs]
                + [pl.BlockSpec((tm, LANES), cur) for _ in lses]
                + [pl.BlockSpec(pts.shape, lambda s: (0, 0, 0)),
                   pl.BlockSpec((k, d), lambda s: (0, 0), pipeline_mode=pl.Buffered(1)),
                   pl.BlockSpec((tm, d), prev),
                   pl.BlockSpec((None, 1, d),
                                lambda s: (jnp.maximum(s - 1, 0) // tiles_per_seq, 0, 0))])
    return pl.pallas_call(
        functools.partial(_attn_out_kernel, dilations=dilations),
        out_shape=jax.ShapeDtypeStruct((m, d), F32),
        grid=(n_tiles + 1,),
        in_specs=in_specs,
        out_specs=pl.BlockSpec((tm, d), prev),
        scratch_shapes=[pltpu.VMEM((tm, k), BF16), pltpu.VMEM((tm, k), BF16),
                        pltpu.VMEM((tm, LANES), F32), pltpu.VMEM((tm, LANES), F32)],
        compiler_params=_params(1),
        name="attn_out_project",
    )(*outs, *lses, pts, w, x2, gate)


def _out_kernel(a_ref, w_ref, x_ref, gate_ref, out_ref):
    y = jnp.dot(a_ref[...], w_ref[...], preferred_element_type=F32)
    out_ref[...] = x_ref[...] + gate_ref[...] * y


def _out_project(x2, seq, gate, w, a):
    m, d = x2.shape
    k = w.shape[0]
    tm, tn = OUT_ROW_TILE, OUT_COL_TILE
    tiles_per_seq = seq // tm
    return pl.pallas_call(
        _out_kernel,
        out_shape=jax.ShapeDtypeStruct((m, d), F32),
        grid=(m // tm, d // tn),
        in_specs=[
            pl.BlockSpec((tm, k), lambda i, j: (i, 0)),
            pl.BlockSpec((k, tn), lambda i, j: (0, j)),
            pl.BlockSpec((tm, tn), lambda i, j: (i, j)),
            pl.BlockSpec((None, 1, tn), lambda i, j: (i // tiles_per_seq, 0, j)),
        ],
        out_specs=pl.BlockSpec((tm, tn), lambda i, j: (i, j)),
        compiler_params=_params(2),
        name="out_project",
    )(a, w, x2, gate)


def _retention_kernel(q_ref, k_ref, v_ref, g_ref, decay_ref, xi_ref, zeta_ref, gamc_ref,
                      gn_ref, o_ref, r_ref, *, tc):
    c = RET_CHUNK

    @pl.when(pl.program_id(2) == 0)
    def _():
        r_ref[...] = jnp.zeros_like(r_ref)

    decay = decay_ref[...]
    xi = xi_ref[...]
    zeta = zeta_ref[...]
    gamc = gamc_ref[...]
    gn = gn_ref[...]

    for ci in range(tc // c):
        rs = slice(ci * c, (ci + 1) * c)
        q = q_ref[rs, :]
        k = k_ref[rs, :]
        v = v_ref[rs, :]
        s = lax.dot_general(q, k, (((1,), (1,)), ((), ())), preferred_element_type=F32) * decay
        inner = jnp.dot(s.astype(BF16), v, preferred_element_type=F32)
        r = r_ref[...]
        cross = jnp.dot(q, r.astype(BF16), preferred_element_type=F32) * xi
        kz = (k.astype(F32) * zeta).astype(BF16)
        kv = lax.dot_general(kz, v, (((0,), (0,)), ((), ())), preferred_element_type=F32)
        r_ref[...] = gamc * r + kv
        y = inner + cross
        mu = jnp.mean(y, axis=-1, keepdims=True)
        yc = y - mu
        var = jnp.mean(yc * yc, axis=-1, keepdims=True)
        yn = yc * lax.rsqrt(var + EPS) * gn
        g = g_ref[rs, :].astype(F32)
        o_ref[rs, :] = (_silu(g) * yn).astype(o_ref.dtype)


def _rotary_tables(seq, dk):
    half = dk // 2
    pos = jnp.arange(seq, dtype=F32)
    freqs = ROPE_BASE ** (-jnp.arange(half, dtype=F32) / half)
    ang = pos[:, None] * freqs[None, :]
    cos, sin = jnp.cos(ang), jnp.sin(ang)
    kscale = dk ** -0.5
    return (jnp.stack([cos, cos * kscale, jnp.ones_like(cos)]),
            jnp.stack([sin, sin * kscale, jnp.zeros_like(sin)]))


def _retention_tables():
    h, c = N_HEADS_B, RET_CHUNK
    log_gamma = jnp.log1p(-jnp.exp2(-5.0 - jnp.arange(h, dtype=F32)))
    idx = jnp.arange(c, dtype=F32)
    rel = idx[:, None] - idx[None, :]
    decay = jnp.where(rel >= 0, jnp.exp(log_gamma[:, None, None] * jnp.maximum(rel, 0.0)), 0.0)
    xi = jnp.exp(log_gamma[:, None] * (idx + 1.0))[:, :, None]
    zeta = jnp.exp(log_gamma[:, None] * (c - 1.0 - idx))[:, :, None]
    gamc = jnp.exp(log_gamma * c)[:, None, None]
    return decay, xi, zeta, gamc


def _retention(proj, gn_g):
    b, s, n = proj.shape
    h = N_HEADS_B
    dv = gn_g.shape[-1] // h
    dk = (n - 2 * h * dv) // (2 * h)
    tc = RET_ROW_TILE
    c = RET_CHUNK
    decay, xi, zeta, gamc = _retention_tables()
    v_blk0 = (2 * h * dk) // dv
    out = pl.pallas_call(
        functools.partial(_retention_kernel, tc=tc),
        out_shape=jax.ShapeDtypeStruct((b, s, h * dv), BF16),
        grid=(b, h, s // tc),
        in_specs=[
            pl.BlockSpec((None, tc, dk), lambda bi, hi, t: (bi, t, hi)),
            pl.BlockSpec((None, tc, dk), lambda bi, hi, t: (bi, t, h + hi)),
            pl.BlockSpec((None, tc, dv), lambda bi, hi, t: (bi, t, v_blk0 + hi)),
            pl.BlockSpec((None, tc, dv), lambda bi, hi, t: (bi, t, v_blk0 + h + hi)),
            pl.BlockSpec((None, c, c), lambda bi, hi, t: (hi, 0, 0)),
            pl.BlockSpec((None, c, 1), lambda bi, hi, t: (hi, 0, 0)),
            pl.BlockSpec((None, c, 1), lambda bi, hi, t: (hi, 0, 0)),
            pl.BlockSpec((None, 1, 1), lambda bi, hi, t: (hi, 0, 0)),
            pl.BlockSpec((1, dv), lambda bi, hi, t: (0, hi)),
        ],
        out_specs=pl.BlockSpec((None, tc, dv), lambda bi, hi, t: (bi, t, hi)),
        scratch_shapes=[pltpu.VMEM((dk, dv), F32)],
        compiler_params=_params(3),
        name="retention",
    )(proj, proj, proj, proj, decay, xi, zeta, gamc, gn_g.reshape(1, h * dv))
    return out.reshape(b * s, h * dv)


def _ffn_kernel(x_ref, gain_ref, shift_ref, scale_ref, gate_ref, wa_ref, wb_ref,
                conv_ref, wd_ref, fg_ref, o_ref, h_ref,
                ua_ref, ub_ref, carry_ref, *, tiles_per_seq, final_norm):
    i = pl.program_id(0)
    f = pl.program_id(1)
    pad = CONV_HALO_ROWS
    tm = x_ref.shape[0]
    u_refs = (ua_ref, ub_ref)

    @pl.when(f == 0)
    def _():
        _norm_mod_rows(x_ref, gain_ref, shift_ref, scale_ref, h_ref)
        o_ref[...] = jnp.zeros_like(o_ref)

    @pl.when(i % tiles_per_seq == 0)
    def _():
        for u_ref in u_refs:
            u_ref[0:pad, :] = jnp.zeros((pad, u_ref.shape[1]), F32)

    @pl.when(i % tiles_per_seq != 0)
    def _():
        for br, u_ref in enumerate(u_refs):
            u_ref[0:pad, :] = carry_ref[f, br]

    def up_project(r0, r1):
        hr = h_ref[r0:r1, :]
        for u_ref, w_ref in zip(u_refs, (wa_ref, wb_ref)):
            u_ref[pad + r0:pad + r1, :] = jnp.dot(hr, w_ref[...], preferred_element_type=F32)

    nf = pl.num_programs(1)
    conv_params = (conv_ref[f], conv_ref[nf + f])

    def conv(u_ref, cp, r0, r1):
        y = cp[2:3, :] * u_ref[pad + r0:pad + r1, :]
        y = y + cp[1:2, :] * u_ref[pad - 1 + r0:pad - 1 + r1, :]
        y = y + cp[0:1, :] * u_ref[pad - 2 + r0:pad - 2 + r1, :]
        return y + cp[CONV_WIDTH:CONV_WIDTH + 1, :]

    def down_project(r0, r1):
        a = conv(ua_ref, conv_params[0], r0, r1)
        b = conv(ub_ref, conv_params[1], r0, r1)
        act = (_silu(a) * b).astype(BF16)
        o_ref[r0:r1, :] += jnp.dot(act, wd_ref[...], preferred_element_type=F32)

    bounds = list(range(0, tm + 1, FFN_ROW_CHUNK))
    chunks = list(zip(bounds[:-1], bounds[1:]))
    up_project(*chunks[0])
    for ci, (r0, r1) in enumerate(chunks):
        if ci + 1 < len(chunks):
            up_project(*chunks[ci + 1])
        down_project(r0, r1)

    for br, u_ref in enumerate(u_refs):
        carry_ref[f, br] = u_ref[tm:tm + pad, :]

    @pl.when(f == pl.num_programs(1) - 1)
    def _():
        xn = x_ref[...] + gate_ref[...] * o_ref[...]
        if final_norm:
            xn = _rmsnorm(xn, fg_ref[...])
        o_ref[...] = xn


def _pack_conv_params(conv_w, conv_b, tf):
    depth = conv_w.shape[0]
    blocks = conv_w.shape[-1] // tf
    rows = jnp.concatenate([conv_w.reshape(depth, CONV_WIDTH, blocks, tf),
                            conv_b.reshape(depth, 1, blocks, tf)], axis=1)
    rows = jnp.pad(rows, ((0, 0), (0, CONV_HALO_ROWS - CONV_WIDTH - 1), (0, 0), (0, 0)))
    return rows.transpose(0, 2, 1, 3)


def _conv_ffn(x2, seq, gain, shift, scale, gate, w_up, conv_pack, w_down, final_gain,
              layer, final_norm):
    m, d = x2.shape
    ff = w_down.shape[1]
    tm, tf = FFN_ROW_TILE, FFN_COL_TILE
    nf = ff // tf
    tiles_per_seq = seq // tm
    batch = lambda i, f: (i // tiles_per_seq, 0, 0)
    const = lambda i, f: (0, 0)
    return pl.pallas_call(
        functools.partial(_ffn_kernel, tiles_per_seq=tiles_per_seq, final_norm=final_norm),
        out_shape=jax.ShapeDtypeStruct((m, d), F32),
        grid=(m // tm, nf),
        in_specs=[
            pl.BlockSpec((tm, d), lambda i, f: (i, 0), pipeline_mode=pl.Buffered(1)),
            pl.BlockSpec((1, d), const),
            pl.BlockSpec((None, 1, d), batch),
            pl.BlockSpec((None, 1, d), batch),
            pl.BlockSpec((None, 1, d), batch),
            pl.BlockSpec((None, d, tf), lambda i, f: (layer, 0, f)),
            pl.BlockSpec((None, d, tf), lambda i, f: (layer, 0, nf + f)),
            pl.BlockSpec((None,) + conv_pack.shape[1:], lambda i, f: (layer, 0, 0, 0)),
            pl.BlockSpec((None, tf, d), lambda i, f: (layer, f, 0)),
            pl.BlockSpec((1, d), const),
        ],
        out_specs=pl.BlockSpec((tm, d), lambda i, f: (i, 0)),
        scratch_shapes=[pltpu.VMEM((tm, d), BF16),
                        pltpu.VMEM((tm + CONV_HALO_ROWS, tf), F32),
                        pltpu.VMEM((tm + CONV_HALO_ROWS, tf), F32),
                        pltpu.VMEM((nf, 2, CONV_HALO_ROWS, tf), F32)],
        compiler_params=_params(2),
        name="conv_ffn",
    )(x2, gain, shift, scale, gate, w_up, w_up, conv_pack, w_down, final_gain)


def kernel(x, c, mod_w, mod_b, norm_mix, w_in_a, w_out_a, w_in_b, gn_b, w_out_b, norm_ffn,
           ffn_up, ffn_conv_w, ffn_conv_b, ffn_down, final_norm):
    b, s, d = x.shape
    depth = mod_w.shape[0]
    mod = _modulation(c, mod_w, mod_b).reshape(depth, b, 6, 1, d)
    x2 = x.reshape(b * s, d)
    dilations = tuple(dl for _, dl in DILATED_GROUPS)
    ffn_up_bf = ffn_up.astype(BF16)
    ffn_down_bf = ffn_down.astype(BF16)
    conv_pack = _pack_conv_params(ffn_conv_w, ffn_conv_b, FFN_COL_TILE)

    for i in range(depth):
        sh_a, sc_a, g_a, sh_m, sc_m, g_m = (mod[i, :, k] for k in range(6))
        gain = norm_mix[i].reshape(1, d)
        j = i // 2
        if i % 2 == 0:
            qkv = _project(x2, s, gain, sh_a, sc_a, w_in_a[j].astype(BF16), dilations)
            outs, lses = [], []
            for g, dil in enumerate(dilations):
                o, lse = _dilated_attention(qkv, s, g, dil)
                outs.append(o)
                lses.append(lse)
            x2 = _attn_out_project(x2, s, g_a, w_out_a[j].astype(BF16), outs, lses, dilations)
        else:
            dv = gn_b.shape[-1] // N_HEADS_B
            qk_cols = (w_in_b.shape[-1] - 2 * N_HEADS_B * dv) // 2
            cos, sin = _rotary_tables(s, qk_cols // N_HEADS_B)
            proj = _project(x2, s, gain, sh_a, sc_a, w_in_b[j].astype(BF16), (1,),
                            rotary=(cos, sin, qk_cols))
            a = _retention(proj.reshape(b, s, -1), gn_b[j])
            x2 = _out_project(x2, s, g_a, w_out_b[j].astype(BF16), a)
        x2 = _conv_ffn(
            x2, s, norm_ffn[i].reshape(1, d), sh_m, sc_m, g_m, ffn_up_bf, conv_pack,
            ffn_down_bf, final_norm.reshape(1, d), layer=i, final_norm=(i == depth - 1))
    return x2.reshape(b, s, d)
```

```python
import functools

import jax
import jax.numpy as jnp
import numpy as np
from jax import lax
from jax.experimental import pallas as pl
from jax.experimental.pallas import tpu as pltpu

F32 = jnp.float32
BF16 = jnp.bfloat16

EPS = 1e-6
LOG2_E = 1.4426950408889634
HEAD_DIM_A = 128
N_HEADS_A = 16
WIDTH_A = N_HEADS_A * HEAD_DIM_A
DILATED_GROUPS = ((128, 1), (512, 4), (2048, 16))
ATTN_STEPS = 128
N_HEADS_B = 8
RET_CHUNK = 256
ROPE_BASE = 10000.0
CONV_WIDTH = 3

VMEM_LIMIT_BYTES = 60 * 1024 * 1024
LANES = 128
CONV_HALO_ROWS = 8
CAST_ROW_ALIGN = 16

PERM_TILE = 256
ATTN_OUT_ROW_TILE = 512
PROJ_ROW_TILE = 1024
PROJ_COL_TILES = (1024, 2048)
PROJ_VMEM_BUDGET_BYTES = 54 * 1024 * 1024
NORM_ROW_CHUNK = 16
NORM_UNROLL = 16
OUT_ROW_TILE = 1024
OUT_COL_TILE = 512
FFN_ROW_TILE = 1024
FFN_COL_TILE = 512
FFN_ROW_CHUNK = 512
ATTN_ROW_TILE = 1024
ATTN_BLOCKS_PER_STEP = 32
RET_ROW_TILE = 1024
MOD_COL_TILE = 1024


def _params(n_axes):
    return pltpu.CompilerParams(
        dimension_semantics=("arbitrary",) * n_axes,
        vmem_limit_bytes=VMEM_LIMIT_BYTES)


def _silu(x):
    h = 0.5 * x
    return h + h * jnp.tanh(h)


def _rmsnorm(x, gain):
    ms = jnp.mean(x * x, axis=-1, keepdims=True)
    return x * lax.rsqrt(ms + EPS) * gain


def _norm_mod_rows(x_ref, gain_ref, shift_ref, scale_ref, h_ref):
    shift = shift_ref[...]
    gain_mod = gain_ref[...] * (1.0 + scale_ref[...])
    rows = NORM_ROW_CHUNK

    def body(k, carry):
        rs = pl.ds(pl.multiple_of(k * rows, rows), rows)
        h_ref[rs, :] = (_rmsnorm(x_ref[rs, :], gain_mod) + shift).astype(h_ref.dtype)
        return carry

    lax.fori_loop(0, x_ref.shape[0] // rows, body, 0, unroll=NORM_UNROLL)


def _class_major_perm(dilation):
    c = PERM_TILE // dilation
    p = np.arange(PERM_TILE)
    m = np.zeros((PERM_TILE, PERM_TILE), np.float32)
    m[p, (p % c) * dilation + p // c] = 1.0
    return m


def _mod_kernel(c_ref, w_ref, b_ref, o_ref):
    c = c_ref[...]
    c_act = (c * jax.nn.sigmoid(c)).astype(BF16)
    o_ref[...] = jnp.dot(c_act, w_ref[...].astype(BF16),
                         preferred_element_type=F32) + b_ref[...]


def _modulation(c, mod_w, mod_b):
    depth, d, n = mod_w.shape
    b = c.shape[0]
    tn = MOD_COL_TILE
    return pl.pallas_call(
        _mod_kernel,
        out_shape=jax.ShapeDtypeStruct((depth, b, n), F32),
        grid=(depth, n // tn),
        in_specs=[
            pl.BlockSpec((b, d), lambda l, j: (0, 0)),
            pl.BlockSpec((None, d, tn), lambda l, j: (l, 0, j)),
            pl.BlockSpec((None, 1, tn), lambda l, j: (l, 0, j)),
        ],
        out_specs=pl.BlockSpec((None, b, tn), lambda l, j: (l, 0, j)),
        compiler_params=_params(2),
        name="modulation",
    )(c, mod_w, mod_b.reshape(depth, 1, n))


def _proj_kernel(*refs, n_groups, col_blocks_per_group, rotary, n_casts):
    x_ref, gain_ref, shift_ref, scale_ref = refs[:4]
    rest = list(refs[4:])
    perm_ref = rest.pop(0) if n_groups > 1 else None
    cos_ref, sin_ref = (rest.pop(0), rest.pop(0)) if rotary else (None, None)
    w_ref = rest.pop(0)
    cast_src = [rest.pop(0) for _ in range(n_casts)]
    o_ref = rest.pop(0)
    cast_dst = [rest.pop(0) for _ in range(n_casts)]
    (h_ref,) = rest
    j = pl.program_id(1)
    tm = x_ref.shape[0]

    @pl.when(j == 0)
    def _():
        _norm_mod_rows(x_ref, gain_ref, shift_ref, scale_ref, h_ref.at[0])
        for g in range(1, n_groups):
            for t0 in range(0, tm, PERM_TILE):
                rs = slice(t0, t0 + PERM_TILE)
                h_ref[g, rs, :] = jnp.dot(perm_ref[g - 1], h_ref[0, rs, :],
                                          preferred_element_type=F32).astype(BF16)

    g = j // col_blocks_per_group
    y = jnp.dot(h_ref[g], w_ref[...], preferred_element_type=F32)
    if rotary:
        cos, sin = cos_ref[...], sin_ref[...]
        half = cos.shape[1]
        for c0 in range(0, y.shape[1], 2 * half):
            y1, y2 = y[:, c0:c0 + half], y[:, c0 + half:c0 + 2 * half]
            o_ref[:, c0:c0 + half] = (y1 * cos - y2 * sin).astype(o_ref.dtype)
            o_ref[:, c0 + half:c0 + 2 * half] = (y1 * sin + y2 * cos).astype(o_ref.dtype)
    else:
        o_ref[...] = y.astype(o_ref.dtype)

    for src, dst in zip(cast_src, cast_dst):
        dst[...] = src[...].astype(dst.dtype)


def _cast_row_block(n_rows, n_steps):
    br = CAST_ROW_ALIGN
    while n_rows % br or n_rows // br > n_steps:
        br *= 2
    return br


def _project(x2, seq, gain, shift, scale, w, dilations, rotary=None, casts=()):
    m, d = x2.shape
    n = w.shape[1]
    tm = PROJ_ROW_TILE
    n_groups = len(dilations)

    def vmem_bytes(tn):
        steps = (m // tm) * (n // tn)
        cast = sum(2 * _cast_row_block(a.shape[0], steps) * a.shape[1] * (4 + 2) for a in casts)
        return (2 * (tm * d * 4 + d * tn * 2 + tm * tn * 2) + n_groups * tm * d * 2
                + tm * tn * 4 + cast)

    tn = max(t for t in PROJ_COL_TILES if vmem_bytes(t) <= PROJ_VMEM_BUDGET_BYTES)
    tiles_per_seq = seq // tm
    batch = lambda i, j: (i // tiles_per_seq, 0, 0)
    in_specs = [
        pl.BlockSpec((tm, d), lambda i, j: (i, 0)),
        pl.BlockSpec((1, d), lambda i, j: (0, 0)),
        pl.BlockSpec((None, 1, d), batch),
        pl.BlockSpec((None, 1, d), batch),
    ]
    args = [x2, gain, shift, scale]
    if n_groups > 1:
        assert dilations[0] == 1
        perms = jnp.asarray(np.stack([_class_major_perm(dl) for dl in dilations[1:]]), BF16)
        in_specs.append(pl.BlockSpec(perms.shape, lambda i, j: (0, 0, 0)))
        args.append(perms)
    if rotary is not None:
        cos, sin, cols_per_kind = rotary
        blocks_per_kind = cols_per_kind // tn
        last_kind = cos.shape[0] - 1
        table = lambda i, j: (jnp.minimum(j // blocks_per_kind, last_kind),
                              i % tiles_per_seq, 0)
        in_specs += [pl.BlockSpec((None, tm, cos.shape[2]), table)] * 2
        args += [cos, sin]
    in_specs.append(pl.BlockSpec((d, tn), lambda i, j: (0, j)))
    args.append(w)
    nj = n // tn
    out_specs = [pl.BlockSpec((tm, tn), lambda i, j: (i, j))]
    out_shapes = [jax.ShapeDtypeStruct((m, n), BF16)]
    for a in casts:
        br = _cast_row_block(a.shape[0], (m // tm) * nj)
        last = a.shape[0] // br - 1
        spec = pl.BlockSpec((br, a.shape[1]),
                            lambda i, j, last=last: (jnp.minimum(i * nj + j, last), 0))
        in_specs.append(spec)
        args.append(a)
        out_specs.append(spec)
        out_shapes.append(jax.ShapeDtypeStruct(a.shape, BF16))
    return pl.pallas_call(
        functools.partial(_proj_kernel, n_groups=n_groups,
                          col_blocks_per_group=n // n_groups // tn,
                          rotary=rotary is not None, n_casts=len(casts)),
        out_shape=out_shapes,
        grid=(m // tm, nj),
        in_specs=in_specs,
        out_specs=out_specs,
        scratch_shapes=[pltpu.VMEM((n_groups, tm, d), BF16)],
        compiler_params=_params(2),
        name="in_project",
    )(*args)


def _row_pieces(c, r0, n):
    pieces = []
    r = r0
    while r < r0 + n:
        stop = min((r // c + 1) * c, r0 + n)
        pieces.append((r // c, r % c, stop - (r // c) * c))
        r = stop
    return pieces


def _load_rows(ref, r0, n, cs):
    parts = [ref[k, a:b, cs] for k, a, b in _row_pieces(ref.shape[1], r0, n)]
    return parts[0] if len(parts) == 1 else jnp.concatenate(parts, axis=0)


def _store_rows(ref, r0, cs, val):
    off = 0
    for k, a, b in _row_pieces(ref.shape[1], r0, val.shape[0]):
        ref[k, a:b, cs] = val[off:off + b - a]
        off += b - a


def _attn_kernel(q_ref, k_ref, v_ref, kp_ref, vp_ref, o_ref, lse_ref, *, tq, hb):
    n = pl.program_id(2)
    hblk = pl.program_id(3)
    w = ATTN_STEPS
    nq = tq // w
    scale = HEAD_DIM_A ** -0.5
    all_lanes = slice(0, LANES)

    @pl.when(hblk == 0)
    def _():
        lse_ref[...] = jnp.zeros_like(lse_ref)

    qi = lax.broadcasted_iota(jnp.int32, (w, 2 * w), 0)
    kj = lax.broadcasted_iota(jnp.int32, (w, 2 * w), 1)
    band = (kj >= qi) & (kj <= qi + w)
    band_first = band & ((kj >= w) | (n > 0))
    lane = lax.broadcasted_iota(jnp.int32, (w, LANES), 1)

    for hh in range(hb):
        cs = slice(hh * HEAD_DIM_A, (hh + 1) * HEAD_DIM_A)
        head = hblk * hb + hh
        for qb in range(nq):
            q = _load_rows(q_ref, qb * w, w, cs)
            if qb == 0:
                kcat = jnp.concatenate([_load_rows(kp_ref, 0, w, cs),
                                        _load_rows(k_ref, 0, w, cs)], axis=0)
                vcat = jnp.concatenate([_load_rows(vp_ref, 0, w, cs),
                                        _load_rows(v_ref, 0, w, cs)], axis=0)
                mask = band_first
            else:
                kcat = _load_rows(k_ref, (qb - 1) * w, 2 * w, cs)
                vcat = _load_rows(v_ref, (qb - 1) * w, 2 * w, cs)
                mask = band
            s = lax.dot_general(q, kcat, (((1,), (1,)), ((), ())),
                                preferred_element_type=F32)
            s = jnp.where(mask, s, -jnp.inf)
            m = jnp.max(s, axis=-1, keepdims=True)
            p = jnp.exp2((s - m) * (scale * LOG2_E))
            den = jnp.sum(p, axis=-1, keepdims=True)
            o = jnp.dot(p.astype(BF16), vcat, preferred_element_type=F32) / den
            _store_rows(o_ref, qb * w, cs, o.astype(o_ref.dtype))
            lse = m * scale + jnp.log(den)
            prev = _load_rows(lse_ref, qb * w, w, all_lanes)
            _store_rows(lse_ref, qb * w, all_lanes, jnp.where(lane == head, lse, prev))


def _dilated_attention(qkv, seq, group, dilation):
    m, n_all = qkv.shape
    b = m // seq
    d = dilation
    c = PERM_TILE // d
    tiles = seq // PERM_TILE
    tq = min(ATTN_ROW_TILE, seq // d)
    tpq = tq // c
    hb = ATTN_BLOCKS_PER_STEP // (tq // ATTN_STEPS)
    cw = hb * HEAD_DIM_A
    ncb = WIDTH_A // cw
    w = ATTN_STEPS
    col0 = group * 3 * ncb
    pc = min(c, w)
    ptiles = w // pc
    rows_view = lambda a: a.reshape(b, tiles, d, c, a.shape[-1])

    def prev_map(section):
        if c >= w:
            return lambda bi, r, nn, h: (bi, jnp.maximum(nn * tpq - 1, 0), r, c // w - 1,
                                         col0 + section * ncb + h)
        return lambda bi, r, nn, h: (bi, jnp.maximum(nn * (tpq // ptiles) - 1, 0), r, 0,
                                     col0 + section * ncb + h)

    def cur_map(section):
        return lambda bi, r, nn, h: (bi, nn, r, 0, col0 + section * ncb + h)

    qv = rows_view(qkv)
    o, lse = pl.pallas_call(
        functools.partial(_attn_kernel, tq=tq, hb=hb),
        out_shape=(jax.ShapeDtypeStruct((b, tiles, d, c, WIDTH_A), BF16),
                   jax.ShapeDtypeStruct((b, tiles, d, c, LANES), F32)),
        grid=(b, d, seq // d // tq, ncb),
        in_specs=[
            pl.BlockSpec((None, tpq, None, c, cw), cur_map(0)),
            pl.BlockSpec((None, tpq, None, c, cw), cur_map(1)),
            pl.BlockSpec((None, tpq, None, c, cw), cur_map(2)),
            pl.BlockSpec((None, ptiles, None, pc, cw), prev_map(1)),
            pl.BlockSpec((None, ptiles, None, pc, cw), prev_map(2)),
        ],
        out_specs=(
            pl.BlockSpec((None, tpq, None, c, cw), lambda bi, r, nn, h: (bi, nn, r, 0, h)),
            pl.BlockSpec((None, tpq, None, c, LANES), lambda bi, r, nn, h: (bi, nn, r, 0, 0)),
        ),
        compiler_params=_params(4),
        name=f"dilated_attention_d{d}",
    )(qv, qv, qv, qv, qv)
    return o.reshape(m, WIDTH_A), lse.reshape(m, LANES)


def _merge_tile(o_refs, lse_refs, pt_ref, lsn_refs, a_ref, dilations):
    perm_tiles = range(0, a_ref.shape[0], PERM_TILE)
    ls = [lse_refs[0][...]]
    for g in range(1, len(dilations)):
        d = dilations[g]
        c = PERM_TILE // d
        for t0 in perm_tiles:
            for r in range(d):
                lsn_refs[g - 1][pl.ds(t0 + r, c, stride=d), :] = (
                    lse_refs[g][t0 + r * c:t0 + (r + 1) * c, :])
        ls.append(lsn_refs[g - 1][...])
    mx = functools.reduce(jnp.maximum, ls)
    es = [jnp.exp(l - mx) for l in ls]
    inv = 1.0 / functools.reduce(lambda u, v: u + v, es)
    alphas = [e * inv for e in es[1:]]
    heads_per_dot = 2
    for hp in range(N_HEADS_A // heads_per_dot):
        cs2 = slice(hp * heads_per_dot * HEAD_DIM_A, (hp + 1) * heads_per_dot * HEAD_DIM_A)
        base = o_refs[0][:, cs2].astype(F32)
        others = [jnp.concatenate(
            [jnp.dot(pt_ref[g - 1], o_refs[g][t0:t0 + PERM_TILE, cs2],
                     preferred_element_type=F32) for t0 in perm_tiles], axis=0)
            for g in range(1, len(dilations))]
        for hh in range(heads_per_dot):
            h = hp * heads_per_dot + hh
            ls_ = slice(hh * HEAD_DIM_A, (hh + 1) * HEAD_DIM_A)
            acc = base[:, ls_]
            for al, og in zip(alphas, others):
                acc = acc + al[:, h:h + 1] * (og[:, ls_] - base[:, ls_])
            a_ref[:, h * HEAD_DIM_A:(h + 1) * HEAD_DIM_A] = acc.astype(a_ref.dtype)


def _attn_out_kernel(o0, o1, o2, l0, l1, l2, pt_ref, w_ref, x_ref, gate_ref, out_ref,
                     a0_ref, a1_ref, lsn1_ref, lsn2_ref, *, dilations):
    s = pl.program_id(0)

    @pl.when(s == 0)
    def _():
        a1_ref[...] = jnp.zeros_like(a1_ref)

    def step(a_new, a_old):
        _merge_tile((o0, o1, o2), (l0, l1, l2), pt_ref, (lsn1_ref, lsn2_ref), a_new, dilations)
        y = jnp.dot(a_old[...], w_ref[...], preferred_element_type=F32)
        out_ref[...] = x_ref[...] + gate_ref[...] * y

    @pl.when(s % 2 == 0)
    def _():
        step(a0_ref, a1_ref)

    @pl.when(s % 2 == 1)
    def _():
        step(a1_ref, a0_ref)


def _attn_out_project(x2, seq, gate, w, outs, lses, dilations):
    m, d = x2.shape
    k = w.shape[0]
    tm = ATTN_OUT_ROW_TILE
    n_tiles = m // tm
    tiles_per_seq = seq // tm
    pts = jnp.asarray(np.stack([_class_major_perm(dl).T for dl in dilations[1:]]), BF16)
    cur = lambda s: (jnp.minimum(s, n_tiles - 1), 0)
    prev = lambda s: (jnp.maximum(s - 1, 0), 0)
    in_specs = ([pl.BlockSpec((tm, k), cur) for _ in outs]
                + [pl.BlockSpec((tm, LANES), cur) for _ in lses]
                + [pl.BlockSpec(pts.shape, lambda s: (0, 0, 0)),
                   pl.BlockSpec((k, d), lambda s: (0, 0), pipeline_mode=pl.Buffered(1)),
                   pl.BlockSpec((tm, d), prev),
                   pl.BlockSpec((None, 1, d),
                                lambda s: (jnp.maximum(s - 1, 0) // tiles_per_seq, 0, 0))])
    return pl.pallas_call(
        functools.partial(_attn_out_kernel, dilations=dilations),
        out_shape=jax.ShapeDtypeStruct((m, d), F32),
        grid=(n_tiles + 1,),
        in_specs=in_specs,
        out_specs=pl.BlockSpec((tm, d), prev),
        scratch_shapes=[pltpu.VMEM((tm, k), BF16), pltpu.VMEM((tm, k), BF16),
                        pltpu.VMEM((tm, LANES), F32), pltpu.VMEM((tm, LANES), F32)],
        compiler_params=_params(1),
        name="attn_out_project",
    )(*outs, *lses, pts, w, x2, gate)


def _out_kernel(a_ref, w_ref, x_ref, gate_ref, out_ref):
    y = jnp.dot(a_ref[...], w_ref[...], preferred_element_type=F32)
    out_ref[...] = x_ref[...] + gate_ref[...] * y


def _out_project(x2, seq, gate, w, a):
    m, d = x2.shape
    k = w.shape[0]
    tm, tn = OUT_ROW_TILE, OUT_COL_TILE
    tiles_per_seq = seq // tm
    return pl.pallas_call(
        _out_kernel,
        out_shape=jax.ShapeDtypeStruct((m, d), F32),
        grid=(m // tm, d // tn),
        in_specs=[
            pl.BlockSpec((tm, k), lambda i, j: (i, 0)),
            pl.BlockSpec((k, tn), lambda i, j: (0, j)),
            pl.BlockSpec((tm, tn), lambda i, j: (i, j)),
            pl.BlockSpec((None, 1, tn), lambda i, j: (i // tiles_per_seq, 0, j)),
        ],
        out_specs=pl.BlockSpec((tm, tn), lambda i, j: (i, j)),
        compiler_params=_params(2),
        name="out_project",
    )(a, w, x2, gate)


def _retention_kernel(q_ref, k_ref, v_ref, g_ref, decay_ref, xi_ref, zeta_ref, gamc_ref,
                      gn_ref, o_ref, r_ref, *, tc):
    c = RET_CHUNK

    @pl.when(pl.program_id(2) == 0)
    def _():
        r_ref[...] = jnp.zeros_like(r_ref)

    decay = decay_ref[...]
    xi = xi_ref[...]
    zeta = zeta_ref[...]
    gamc = gamc_ref[...]
    gn = gn_ref[...]

    for ci in range(tc // c):
        rs = slice(ci * c, (ci + 1) * c)
        q = q_ref[rs, :]
        k = k_ref[rs, :]
        v = v_ref[rs, :]
        s = lax.dot_general(q, k, (((1,), (1,)), ((), ())), preferred_element_type=F32) * decay
        inner = jnp.dot(s.astype(BF16), v, preferred_element_type=F32)
        r = r_ref[...]
        cross = jnp.dot(q, r.astype(BF16), preferred_element_type=F32) * xi
        kz = (k.astype(F32) * zeta).astype(BF16)
        kv = lax.dot_general(kz, v, (((0,), (0,)), ((), ())), preferred_element_type=F32)
        r_ref[...] = gamc * r + kv
        y = inner + cross
        mu = jnp.mean(y, axis=-1, keepdims=True)
        yc = y - mu
        var = jnp.mean(yc * yc, axis=-1, keepdims=True)
        yn = yc * lax.rsqrt(var + EPS) * gn
        g = g_ref[rs, :].astype(F32)
        o_ref[rs, :] = (_silu(g) * yn).astype(o_ref.dtype)


def _rotary_tables(seq, dk):
    half = dk // 2
    pos = jnp.arange(seq, dtype=F32)
    freqs = ROPE_BASE ** (-jnp.arange(half, dtype=F32) / half)
    ang = pos[:, None] * freqs[None, :]
    cos, sin = jnp.cos(ang), jnp.sin(ang)
    kscale = dk ** -0.5
    return (jnp.stack([cos, cos * kscale, jnp.ones_like(cos)]),
            jnp.stack([sin, sin * kscale, jnp.zeros_like(sin)]))


def _retention_tables():
    h, c = N_HEADS_B, RET_CHUNK
    log_gamma = jnp.log1p(-jnp.exp2(-5.0 - jnp.arange(h, dtype=F32)))
    idx = jnp.arange(c, dtype=F32)
    rel = idx[:, None] - idx[None, :]
    decay = jnp.where(rel >= 0, jnp.exp(log_gamma[:, None, None] * jnp.maximum(rel, 0.0)), 0.0)
    xi = jnp.exp(log_gamma[:, None] * (idx + 1.0))[:, :, None]
    zeta = jnp.exp(log_gamma[:, None] * (c - 1.0 - idx))[:, :, None]
    gamc = jnp.exp(log_gamma * c)[:, None, None]
    return decay, xi, zeta, gamc


def _retention(proj, gn_g):
    b, s, n = proj.shape
    h = N_HEADS_B
    dv = gn_g.shape[-1] // h
    dk = (n - 2 * h * dv) // (2 * h)
    tc = RET_ROW_TILE
    c = RET_CHUNK
    decay, xi, zeta, gamc = _retention_tables()
    v_blk0 = (2 * h * dk) // dv
    out = pl.pallas_call(
        functools.partial(_retention_kernel, tc=tc),
        out_shape=jax.ShapeDtypeStruct((b, s, h * dv), BF16),
        grid=(b, h, s // tc),
        in_specs=[
            pl.BlockSpec((None, tc, dk), lambda bi, hi, t: (bi, t, hi)),
            pl.BlockSpec((None, tc, dk), lambda bi, hi, t: (bi, t, h + hi)),
            pl.BlockSpec((None, tc, dv), lambda bi, hi, t: (bi, t, v_blk0 + hi)),
            pl.BlockSpec((None, tc, dv), lambda bi, hi, t: (bi, t, v_blk0 + h + hi)),
            pl.BlockSpec((None, c, c), lambda bi, hi, t: (hi, 0, 0)),
            pl.BlockSpec((None, c, 1), lambda bi, hi, t: (hi, 0, 0)),
            pl.BlockSpec((None, c, 1), lambda bi, hi, t: (hi, 0, 0)),
            pl.BlockSpec((None, 1, 1), lambda bi, hi, t: (hi, 0, 0)),
            pl.BlockSpec((1, dv), lambda bi, hi, t: (0, hi)),
        ],
        out_specs=pl.BlockSpec((None, tc, dv), lambda bi, hi, t: (bi, t, hi)),
        scratch_shapes=[pltpu.VMEM((dk, dv), F32)],
        compiler_params=_params(3),
        name="retention",
    )(proj, proj, proj, proj, decay, xi, zeta, gamc, gn_g.reshape(1, h * dv))
    return out.reshape(b * s, h * dv)


def _ffn_kernel(x_ref, gain_ref, shift_ref, scale_ref, gate_ref, wa_ref, wb_ref,
                conv_ref, wd_ref, fg_ref, o_ref, h_ref,
                ua_ref, ub_ref, carry_ref, *, tiles_per_seq, final_norm):
    i = pl.program_id(0)
    f = pl.program_id(1)
    pad = CONV_HALO_ROWS
    tm = x_ref.shape[0]
    u_refs = (ua_ref, ub_ref)

    @pl.when(f == 0)
    def _():
        _norm_mod_rows(x_ref, gain_ref, shift_ref, scale_ref, h_ref)
        o_ref[...] = jnp.zeros_like(o_ref)

    @pl.when(i % tiles_per_seq == 0)
    def _():
        for u_ref in u_refs:
            u_ref[0:pad, :] = jnp.zeros((pad, u_ref.shape[1]), F32)

    @pl.when(i % tiles_per_seq != 0)
    def _():
        for br, u_ref in enumerate(u_refs):
            u_ref[0:pad, :] = carry_ref[f, br]

    def up_project(r0, r1):
        hr = h_ref[r0:r1, :]
        for u_ref, w_ref in zip(u_refs, (wa_ref, wb_ref)):
            u_ref[pad + r0:pad + r1, :] = jnp.dot(hr, w_ref[...], preferred_element_type=F32)

    nf = pl.num_programs(1)
    conv_params = (conv_ref[f], conv_ref[nf + f])

    def conv(u_ref, cp, r0, r1):
        y = cp[2:3, :] * u_ref[pad + r0:pad + r1, :]
        y = y + cp[1:2, :] * u_ref[pad - 1 + r0:pad - 1 + r1, :]
        y = y + cp[0:1, :] * u_ref[pad - 2 + r0:pad - 2 + r1, :]
        return y + cp[CONV_WIDTH:CONV_WIDTH + 1, :]

    def down_project(r0, r1):
        a = conv(ua_ref, conv_params[0], r0, r1)
        b = conv(ub_ref, conv_params[1], r0, r1)
        act = (_silu(a) * b).astype(BF16)
        o_ref[r0:r1, :] += jnp.dot(act, wd_ref[...], preferred_element_type=F32)

    bounds = list(range(0, tm + 1, FFN_ROW_CHUNK))
    chunks = list(zip(bounds[:-1], bounds[1:]))
    up_project(*chunks[0])
    for ci, (r0, r1) in enumerate(chunks):
        if ci + 1 < len(chunks):
            up_project(*chunks[ci + 1])
        down_project(r0, r1)

    for br, u_ref in enumerate(u_refs):
        carry_ref[f, br] = u_ref[tm:tm + pad, :]

    @pl.when(f == pl.num_programs(1) - 1)
    def _():
        xn = x_ref[...] + gate_ref[...] * o_ref[...]
        if final_norm:
            xn = _rmsnorm(xn, fg_ref[...])
        o_ref[...] = xn


def _pack_conv_params(conv_w, conv_b, tf):
    depth = conv_w.shape[0]
    blocks = conv_w.shape[-1] // tf
    rows = jnp.concatenate([conv_w.reshape(depth, CONV_WIDTH, blocks, tf),
                            conv_b.reshape(depth, 1, blocks, tf)], axis=1)
    rows = jnp.pad(rows, ((0, 0), (0, CONV_HALO_ROWS - CONV_WIDTH - 1), (0, 0), (0, 0)))
    return rows.transpose(0, 2, 1, 3)


def _conv_ffn(x2, seq, gain, shift, scale, gate, w_up, conv_pack, w_down, final_gain,
              layer, final_norm):
    m, d = x2.shape
    ff = w_down.shape[1]
    tm, tf = FFN_ROW_TILE, FFN_COL_TILE
    nf = ff // tf
    tiles_per_seq = seq // tm
    batch = lambda i, f: (i // tiles_per_seq, 0, 0)
    const = lambda i, f: (0, 0)
    return pl.pallas_call(
        functools.partial(_ffn_kernel, tiles_per_seq=tiles_per_seq, final_norm=final_norm),
        out_shape=jax.ShapeDtypeStruct((m, d), F32),
        grid=(m // tm, nf),
        in_specs=[
            pl.BlockSpec((tm, d), lambda i, f: (i, 0), pipeline_mode=pl.Buffered(1)),
            pl.BlockSpec((1, d), const),
            pl.BlockSpec((None, 1, d), batch),
            pl.BlockSpec((None, 1, d), batch),
            pl.BlockSpec((None, 1, d), batch),
            pl.BlockSpec((None, d, tf), lambda i, f: (layer, 0, f)),
            pl.BlockSpec((None, d, tf), lambda i, f: (layer, 0, nf + f)),
            pl.BlockSpec((None,) + conv_pack.shape[1:], lambda i, f: (layer, 0, 0, 0)),
            pl.BlockSpec((None, tf, d), lambda i, f: (layer, f, 0)),
            pl.BlockSpec((1, d), const),
        ],
        out_specs=pl.BlockSpec((tm, d), lambda i, f: (i, 0)),
        scratch_shapes=[pltpu.VMEM((tm, d), BF16),
                        pltpu.VMEM((tm + CONV_HALO_ROWS, tf), F32),
                        pltpu.VMEM((tm + CONV_HALO_ROWS, tf), F32),
                        pltpu.VMEM((nf, 2, CONV_HALO_ROWS, tf), F32)],
        compiler_params=_params(2),
        name="conv_ffn",
    )(x2, gain, shift, scale, gate, w_up, w_up, conv_pack, w_down, final_gain)


def kernel(x, c, mod_w, mod_b, norm_mix, w_in_a, w_out_a, w_in_b, gn_b, w_out_b, norm_ffn,
           ffn_up, ffn_conv_w, ffn_conv_b, ffn_down, final_norm):
    b, s, d = x.shape
    depth = mod_w.shape[0]
    assert depth == 2, "layer 0 = dilated attention mixer, layer 1 = retention mixer"
    mod = _modulation(c, mod_w, mod_b).reshape(depth, b, 6, 1, d)
    x2 = x.reshape(b * s, d)
    dilations = tuple(dl for _, dl in DILATED_GROUPS)
    conv_pack = _pack_conv_params(ffn_conv_w, ffn_conv_b, FFN_COL_TILE)
    later = (w_out_a, w_in_b, w_out_b, ffn_up, ffn_down)
    later_2d = tuple(a.reshape(-1, a.shape[-1]) for a in later)

    for i in range(depth):
        sh_a, sc_a, g_a, sh_m, sc_m, g_m = (mod[i, :, k] for k in range(6))
        gain = norm_mix[i].reshape(1, d)
        j = i // 2
        if i == 0:
            qkv, *cast = _project(x2, s, gain, sh_a, sc_a, w_in_a[j].astype(BF16), dilations,
                                  casts=later_2d)
            w_out_a_bf, w_in_b_bf, w_out_b_bf, ffn_up_bf, ffn_down_bf = (
                cb.reshape(a.shape) for cb, a in zip(cast, later))
            outs, lses = [], []
            for g, dil in enumerate(dilations):
                o, lse = _dilated_attention(qkv, s, g, dil)
                outs.append(o)
                lses.append(lse)
            x2 = _attn_out_project(x2, s, g_a, w_out_a_bf[j], outs, lses, dilations)
        else:
            dv = gn_b.shape[-1] // N_HEADS_B
            qk_cols = (w_in_b.shape[-1] - 2 * N_HEADS_B * dv) // 2
            cos, sin = _rotary_tables(s, qk_cols // N_HEADS_B)
            (proj,) = _project(x2, s, gain, sh_a, sc_a, w_in_b_bf[j], (1,),
                               rotary=(cos, sin, qk_cols))
            a = _retention(proj.reshape(b, s, -1), gn_b[j])
            x2 = _out_project(x2, s, g_a, w_out_b_bf[j], a)
        x2 = _conv_ffn(
            x2, s, norm_ffn[i].reshape(1, d), sh_m, sc_m, g_m, ffn_up_bf, conv_pack,
            ffn_down_bf, final_norm.reshape(1, d), layer=i, final_norm=(i == depth - 1))
    return x2.reshape(b, s, d)
```

```python
import functools

import jax
import jax.numpy as jnp
import numpy as np
from jax import lax
from jax.experimental import pallas as pl
from jax.experimental.pallas import tpu as pltpu

F32 = jnp.float32
BF16 = jnp.bfloat16

EPS = 1e-6
LOG2_E = 1.4426950408889634
HEAD_DIM_A = 128
N_HEADS_A = 16
WIDTH_A = N_HEADS_A * HEAD_DIM_A
DILATED_GROUPS = ((128, 1), (512, 4), (2048, 16))
ATTN_STEPS = 128
N_HEADS_B = 8
RET_CHUNK = 256
ROPE_BASE = 10000.0
CONV_WIDTH = 3

VMEM_LIMIT_BYTES = 60 * 1024 * 1024
LANES = 128
CONV_HALO_ROWS = 8
CAST_ROW_ALIGN = 16

PERM_TILE = 256
ATTN_OUT_ROW_TILE = 512
PROJ_ROW_TILE = 1024
PROJ_COL_TILES = (1024, 2048)
PROJ_VMEM_BUDGET_BYTES = 54 * 1024 * 1024
NORM_ROW_CHUNK = 16
NORM_UNROLL = 16
OUT_ROW_TILE = 1024
OUT_COL_TILE = 1024
FFN_ROW_TILE = 1024
FFN_COL_TILE = 512
FFN_ROW_CHUNK = 512
ATTN_ROW_TILE = 1024
ATTN_BLOCKS_PER_STEP = 32
RET_ROW_TILE = 2048
MOD_COL_TILE = 1024


def _params(n_axes):
    return pltpu.CompilerParams(
        dimension_semantics=("arbitrary",) * n_axes,
        vmem_limit_bytes=VMEM_LIMIT_BYTES)


def _silu(x):
    h = 0.5 * x
    return h + h * jnp.tanh(h)


def _rmsnorm(x, gain):
    ms = jnp.mean(x * x, axis=-1, keepdims=True)
    return x * lax.rsqrt(ms + EPS) * gain


def _norm_mod_rows(x_ref, gain_ref, shift_ref, scale_ref, h_ref):
    shift = shift_ref[...]
    gain_mod = gain_ref[...] * (1.0 + scale_ref[...])
    rows = NORM_ROW_CHUNK

    def body(k, carry):
        rs = pl.ds(pl.multiple_of(k * rows, rows), rows)
        h_ref[rs, :] = (_rmsnorm(x_ref[rs, :], gain_mod) + shift).astype(h_ref.dtype)
        return carry

    lax.fori_loop(0, x_ref.shape[0] // rows, body, 0, unroll=NORM_UNROLL)


def _class_major_perm(dilation):
    c = PERM_TILE // dilation
    p = np.arange(PERM_TILE)
    m = np.zeros((PERM_TILE, PERM_TILE), np.float32)
    m[p, (p % c) * dilation + p // c] = 1.0
    return m


def _mod_kernel(c_ref, w_ref, b_ref, o_ref):
    c = c_ref[...]
    c_act = (c * jax.nn.sigmoid(c)).astype(BF16)
    o_ref[...] = jnp.dot(c_act, w_ref[...].astype(BF16),
                         preferred_element_type=F32) + b_ref[...]


def _modulation(c, mod_w, mod_b):
    depth, d, n = mod_w.shape
    b = c.shape[0]
    tn = MOD_COL_TILE
    return pl.pallas_call(
        _mod_kernel,
        out_shape=jax.ShapeDtypeStruct((depth, b, n), F32),
        grid=(depth, n // tn),
        in_specs=[
            pl.BlockSpec((b, d), lambda l, j: (0, 0)),
            pl.BlockSpec((None, d, tn), lambda l, j: (l, 0, j)),
            pl.BlockSpec((None, 1, tn), lambda l, j: (l, 0, j)),
        ],
        out_specs=pl.BlockSpec((None, b, tn), lambda l, j: (l, 0, j)),
        compiler_params=_params(2),
        name="modulation",
    )(c, mod_w, mod_b.reshape(depth, 1, n))


def _proj_kernel(*refs, n_groups, col_blocks_per_group, rotary, n_casts):
    x_ref, gain_ref, shift_ref, scale_ref = refs[:4]
    rest = list(refs[4:])
    perm_ref = rest.pop(0) if n_groups > 1 else None
    cos_ref, sin_ref = (rest.pop(0), rest.pop(0)) if rotary else (None, None)
    w_ref = rest.pop(0)
    cast_src = [rest.pop(0) for _ in range(n_casts)]
    o_ref = rest.pop(0)
    cast_dst = [rest.pop(0) for _ in range(n_casts)]
    (h_ref,) = rest
    j = pl.program_id(1)
    tm = x_ref.shape[0]

    @pl.when(j == 0)
    def _():
        _norm_mod_rows(x_ref, gain_ref, shift_ref, scale_ref, h_ref.at[0])
        for g in range(1, n_groups):
            for t0 in range(0, tm, PERM_TILE):
                rs = slice(t0, t0 + PERM_TILE)
                h_ref[g, rs, :] = jnp.dot(perm_ref[g - 1], h_ref[0, rs, :],
                                          preferred_element_type=F32).astype(BF16)

    g = j // col_blocks_per_group
    y = jnp.dot(h_ref[g], w_ref[...], preferred_element_type=F32)
    if rotary:
        cos, sin = cos_ref[...], sin_ref[...]
        half = cos.shape[1]
        for c0 in range(0, y.shape[1], 2 * half):
            y1, y2 = y[:, c0:c0 + half], y[:, c0 + half:c0 + 2 * half]
            o_ref[:, c0:c0 + half] = (y1 * cos - y2 * sin).astype(o_ref.dtype)
            o_ref[:, c0 + half:c0 + 2 * half] = (y1 * sin + y2 * cos).astype(o_ref.dtype)
    else:
        o_ref[...] = y.astype(o_ref.dtype)

    for src, dst in zip(cast_src, cast_dst):
        dst[...] = src[...].astype(dst.dtype)


def _cast_row_block(n_rows, n_steps):
    br = CAST_ROW_ALIGN
    while n_rows % br or n_rows // br > n_steps:
        br *= 2
    return br


def _project(x2, seq, gain, shift, scale, w, dilations, rotary=None, casts=()):
    m, d = x2.shape
    n = w.shape[1]
    tm = PROJ_ROW_TILE
    n_groups = len(dilations)

    def vmem_bytes(tn):
        steps = (m // tm) * (n // tn)
        cast = sum(2 * _cast_row_block(a.shape[0], steps) * a.shape[1] * (4 + 2) for a in casts)
        return (2 * (tm * d * 4 + d * tn * 2 + tm * tn * 2) + n_groups * tm * d * 2
                + tm * tn * 4 + cast)

    tn = max(t for t in PROJ_COL_TILES if vmem_bytes(t) <= PROJ_VMEM_BUDGET_BYTES)
    tiles_per_seq = seq // tm
    batch = lambda i, j: (i // tiles_per_seq, 0, 0)
    in_specs = [
        pl.BlockSpec((tm, d), lambda i, j: (i, 0)),
        pl.BlockSpec((1, d), lambda i, j: (0, 0)),
        pl.BlockSpec((None, 1, d), batch),
        pl.BlockSpec((None, 1, d), batch),
    ]
    args = [x2, gain, shift, scale]
    if n_groups > 1:
        assert dilations[0] == 1
        perms = jnp.asarray(np.stack([_class_major_perm(dl) for dl in dilations[1:]]), BF16)
        in_specs.append(pl.BlockSpec(perms.shape, lambda i, j: (0, 0, 0)))
        args.append(perms)
    if rotary is not None:
        cos, sin, cols_per_kind = rotary
        blocks_per_kind = cols_per_kind // tn
        last_kind = cos.shape[0] - 1
        table = lambda i, j: (jnp.minimum(j // blocks_per_kind, last_kind),
                              i % tiles_per_seq, 0)
        in_specs += [pl.BlockSpec((None, tm, cos.shape[2]), table)] * 2
        args += [cos, sin]
    in_specs.append(pl.BlockSpec((d, tn), lambda i, j: (0, j)))
    args.append(w)
    nj = n // tn
    out_specs = [pl.BlockSpec((tm, tn), lambda i, j: (i, j))]
    out_shapes = [jax.ShapeDtypeStruct((m, n), BF16)]
    for a in casts:
        br = _cast_row_block(a.shape[0], (m // tm) * nj)
        last = a.shape[0] // br - 1
        spec = pl.BlockSpec((br, a.shape[1]),
                            lambda i, j, last=last: (jnp.minimum(i * nj + j, last), 0))
        in_specs.append(spec)
        args.append(a)
        out_specs.append(spec)
        out_shapes.append(jax.ShapeDtypeStruct(a.shape, BF16))
    return pl.pallas_call(
        functools.partial(_proj_kernel, n_groups=n_groups,
                          col_blocks_per_group=n // n_groups // tn,
                          rotary=rotary is not None, n_casts=len(casts)),
        out_shape=out_shapes,
        grid=(m // tm, nj),
        in_specs=in_specs,
        out_specs=out_specs,
        scratch_shapes=[pltpu.VMEM((n_groups, tm, d), BF16)],
        compiler_params=_params(2),
        name="in_project",
    )(*args)


def _row_pieces(c, r0, n):
    pieces = []
    r = r0
    while r < r0 + n:
        stop = min((r // c + 1) * c, r0 + n)
        pieces.append((r // c, r % c, stop - (r // c) * c))
        r = stop
    return pieces


def _load_rows(ref, r0, n, cs):
    parts = [ref[k, a:b, cs] for k, a, b in _row_pieces(ref.shape[1], r0, n)]
    return parts[0] if len(parts) == 1 else jnp.concatenate(parts, axis=0)


def _store_rows(ref, r0, cs, val):
    off = 0
    for k, a, b in _row_pieces(ref.shape[1], r0, val.shape[0]):
        ref[k, a:b, cs] = val[off:off + b - a]
        off += b - a


def _attn_kernel(q_ref, k_ref, v_ref, kp_ref, vp_ref, o_ref, lse_ref, *, tq, hb):
    n = pl.program_id(2)
    hblk = pl.program_id(3)
    w = ATTN_STEPS
    nq = tq // w
    scale = HEAD_DIM_A ** -0.5
    all_lanes = slice(0, LANES)

    @pl.when(hblk == 0)
    def _():
        lse_ref[...] = jnp.zeros_like(lse_ref)

    qi = lax.broadcasted_iota(jnp.int32, (w, 2 * w), 0)
    kj = lax.broadcasted_iota(jnp.int32, (w, 2 * w), 1)
    band = (kj >= qi) & (kj <= qi + w)
    band_first = band & ((kj >= w) | (n > 0))
    lane = lax.broadcasted_iota(jnp.int32, (w, LANES), 1)

    for hh in range(hb):
        cs = slice(hh * HEAD_DIM_A, (hh + 1) * HEAD_DIM_A)
        head = hblk * hb + hh
        for qb in range(nq):
            q = _load_rows(q_ref, qb * w, w, cs)
            if qb == 0:
                kcat = jnp.concatenate([_load_rows(kp_ref, 0, w, cs),
                                        _load_rows(k_ref, 0, w, cs)], axis=0)
                vcat = jnp.concatenate([_load_rows(vp_ref, 0, w, cs),
                                        _load_rows(v_ref, 0, w, cs)], axis=0)
                mask = band_first
            else:
                kcat = _load_rows(k_ref, (qb - 1) * w, 2 * w, cs)
                vcat = _load_rows(v_ref, (qb - 1) * w, 2 * w, cs)
                mask = band
            s = lax.dot_general(q, kcat, (((1,), (1,)), ((), ())),
                                preferred_element_type=F32)
            s = jnp.where(mask, s, -jnp.inf)
            m = jnp.max(s, axis=-1, keepdims=True)
            p = jnp.exp2((s - m) * (scale * LOG2_E))
            den = jnp.sum(p, axis=-1, keepdims=True)
            o = jnp.dot(p.astype(BF16), vcat, preferred_element_type=F32) / den
            _store_rows(o_ref, qb * w, cs, o.astype(o_ref.dtype))
            lse = m * scale + jnp.log(den)
            prev = _load_rows(lse_ref, qb * w, w, all_lanes)
            _store_rows(lse_ref, qb * w, all_lanes, jnp.where(lane == head, lse, prev))


def _dilated_attention(qkv, seq, group, dilation):
    m, n_all = qkv.shape
    b = m // seq
    d = dilation
    c = PERM_TILE // d
    tiles = seq // PERM_TILE
    tq = min(ATTN_ROW_TILE, seq // d)
    tpq = tq // c
    hb = ATTN_BLOCKS_PER_STEP // (tq // ATTN_STEPS)
    cw = hb * HEAD_DIM_A
    ncb = WIDTH_A // cw
    w = ATTN_STEPS
    col0 = group * 3 * ncb
    pc = min(c, w)
    ptiles = w // pc
    rows_view = lambda a: a.reshape(b, tiles, d, c, a.shape[-1])

    def prev_map(section):
        if c >= w:
            return lambda bi, r, nn, h: (bi, jnp.maximum(nn * tpq - 1, 0), r, c // w - 1,
                                         col0 + section * ncb + h)
        return lambda bi, r, nn, h: (bi, jnp.maximum(nn * (tpq // ptiles) - 1, 0), r, 0,
                                     col0 + section * ncb + h)

    def cur_map(section):
        return lambda bi, r, nn, h: (bi, nn, r, 0, col0 + section * ncb + h)

    qv = rows_view(qkv)
    o, lse = pl.pallas_call(
        functools.partial(_attn_kernel, tq=tq, hb=hb),
        out_shape=(jax.ShapeDtypeStruct((b, tiles, d, c, WIDTH_A), BF16),
                   jax.ShapeDtypeStruct((b, tiles, d, c, LANES), F32)),
        grid=(b, d, seq // d // tq, ncb),
        in_specs=[
            pl.BlockSpec((None, tpq, None, c, cw), cur_map(0)),
            pl.BlockSpec((None, tpq, None, c, cw), cur_map(1)),
            pl.BlockSpec((None, tpq, None, c, cw), cur_map(2)),
            pl.BlockSpec((None, ptiles, None, pc, cw), prev_map(1)),
            pl.BlockSpec((None, ptiles, None, pc, cw), prev_map(2)),
        ],
        out_specs=(
            pl.BlockSpec((None, tpq, None, c, cw), lambda bi, r, nn, h: (bi, nn, r, 0, h)),
            pl.BlockSpec((None, tpq, None, c, LANES), lambda bi, r, nn, h: (bi, nn, r, 0, 0)),
        ),
        compiler_params=_params(4),
        name=f"dilated_attention_d{d}",
    )(qv, qv, qv, qv, qv)
    return o.reshape(m, WIDTH_A), lse.reshape(m, LANES)


def _merge_tile(o_refs, lse_refs, pt_ref, lsn_refs, a_ref, dilations):
    perm_tiles = range(0, a_ref.shape[0], PERM_TILE)
    ls = [lse_refs[0][...]]
    for g in range(1, len(dilations)):
        d = dilations[g]
        c = PERM_TILE // d
        for t0 in perm_tiles:
            for r in range(d):
                lsn_refs[g - 1][pl.ds(t0 + r, c, stride=d), :] = (
                    lse_refs[g][t0 + r * c:t0 + (r + 1) * c, :])
        ls.append(lsn_refs[g - 1][...])
    mx = functools.reduce(jnp.maximum, ls)
    es = [jnp.exp(l - mx) for l in ls]
    inv = 1.0 / functools.reduce(lambda u, v: u + v, es)
    alphas = [e * inv for e in es[1:]]
    heads_per_dot = 2
    for hp in range(N_HEADS_A // heads_per_dot):
        cs2 = slice(hp * heads_per_dot * HEAD_DIM_A, (hp + 1) * heads_per_dot * HEAD_DIM_A)
        base = o_refs[0][:, cs2].astype(F32)
        others = [jnp.concatenate(
            [jnp.dot(pt_ref[g - 1], o_refs[g][t0:t0 + PERM_TILE, cs2],
                     preferred_element_type=F32) for t0 in perm_tiles], axis=0)
            for g in range(1, len(dilations))]
        for hh in range(heads_per_dot):
            h = hp * heads_per_dot + hh
            ls_ = slice(hh * HEAD_DIM_A, (hh + 1) * HEAD_DIM_A)
            acc = base[:, ls_]
            for al, og in zip(alphas, others):
                acc = acc + al[:, h:h + 1] * (og[:, ls_] - base[:, ls_])
            a_ref[:, h * HEAD_DIM_A:(h + 1) * HEAD_DIM_A] = acc.astype(a_ref.dtype)


def _attn_out_kernel(o0, o1, o2, l0, l1, l2, pt_ref, w_ref, x_ref, gate_ref, out_ref,
                     a0_ref, a1_ref, lsn1_ref, lsn2_ref, *, dilations):
    s = pl.program_id(0)

    @pl.when(s == 0)
    def _():
        a1_ref[...] = jnp.zeros_like(a1_ref)

    def step(a_new, a_old):
        _merge_tile((o0, o1, o2), (l0, l1, l2), pt_ref, (lsn1_ref, lsn2_ref), a_new, dilations)
        y = jnp.dot(a_old[...], w_ref[...], preferred_element_type=F32)
        out_ref[...] = x_ref[...] + gate_ref[...] * y

    @pl.when(s % 2 == 0)
    def _():
        step(a0_ref, a1_ref)

    @pl.when(s % 2 == 1)
    def _():
        step(a1_ref, a0_ref)


def _attn_out_project(x2, seq, gate, w, outs, lses, dilations):
    m, d = x2.shape
    k = w.shape[0]
    tm = ATTN_OUT_ROW_TILE
    n_tiles = m // tm
    tiles_per_seq = seq // tm
    pts = jnp.asarray(np.stack([_class_major_perm(dl).T for dl in dilations[1:]]), BF16)
    cur = lambda s: (jnp.minimum(s, n_tiles - 1), 0)
    prev = lambda s: (jnp.maximum(s - 1, 0), 0)
    in_specs = ([pl.BlockSpec((tm, k), cur) for _ in outs]
                + [pl.BlockSpec((tm, LANES), cur) for _ in lses]
                + [pl.BlockSpec(pts.shape, lambda s: (0, 0, 0)),
                   pl.BlockSpec((k, d), lambda s: (0, 0), pipeline_mode=pl.Buffered(1)),
                   pl.BlockSpec((tm, d), prev),
                   pl.BlockSpec((None, 1, d),
                                lambda s: (jnp.maximum(s - 1, 0) // tiles_per_seq, 0, 0))])
    return pl.pallas_call(
        functools.partial(_attn_out_kernel, dilations=dilations),
        out_shape=jax.ShapeDtypeStruct((m, d), F32),
        grid=(n_tiles + 1,),
        in_specs=in_specs,
        out_specs=pl.BlockSpec((tm, d), prev),
        scratch_shapes=[pltpu.VMEM((tm, k), BF16), pltpu.VMEM((tm, k), BF16),
                        pltpu.VMEM((tm, LANES), F32), pltpu.VMEM((tm, LANES), F32)],
        compiler_params=_params(1),
        name="attn_out_project",
    )(*outs, *lses, pts, w, x2, gate)


def _out_kernel(a_ref, w_ref, x_ref, gate_ref, out_ref):
    y = jnp.dot(a_ref[...], w_ref[...], preferred_element_type=F32)
    out_ref[...] = x_ref[...] + gate_ref[...] * y


def _out_project(x2, seq, gate, w, a):
    m, d = x2.shape
    k = w.shape[0]
    tm, tn = OUT_ROW_TILE, OUT_COL_TILE
    tiles_per_seq = seq // tm
    return pl.pallas_call(
        _out_kernel,
        out_shape=jax.ShapeDtypeStruct((m, d), F32),
        grid=(m // tm, d // tn),
        in_specs=[
            pl.BlockSpec((tm, k), lambda i, j: (i, 0)),
            pl.BlockSpec((k, tn), lambda i, j: (0, j)),
            pl.BlockSpec((tm, tn), lambda i, j: (i, j)),
            pl.BlockSpec((None, 1, tn), lambda i, j: (i // tiles_per_seq, 0, j)),
        ],
        out_specs=pl.BlockSpec((tm, tn), lambda i, j: (i, j)),
        compiler_params=_params(2),
        name="out_project",
    )(a, w, x2, gate)


def _retention_kernel(q_ref, k_ref, v_ref, g_ref, decay_ref, xi_ref, zeta_ref, gamc_ref,
                      gn_ref, o_ref, r_ref, *, tc):
    c = RET_CHUNK

    @pl.when(pl.program_id(2) == 0)
    def _():
        r_ref[...] = jnp.zeros_like(r_ref)

    decay = decay_ref[...]
    xi = xi_ref[...]
    zeta = zeta_ref[...]
    gamc = gamc_ref[...]
    gn = gn_ref[...]

    for ci in range(tc // c):
        rs = slice(ci * c, (ci + 1) * c)
        q = q_ref[rs, :]
        k = k_ref[rs, :]
        v = v_ref[rs, :]
        s = lax.dot_general(q, k, (((1,), (1,)), ((), ())), preferred_element_type=F32) * decay
        inner = jnp.dot(s.astype(BF16), v, preferred_element_type=F32)
        r = r_ref[...]
        cross = jnp.dot(q, r.astype(BF16), preferred_element_type=F32) * xi
        kz = (k.astype(F32) * zeta).astype(BF16)
        kv = lax.dot_general(kz, v, (((0,), (0,)), ((), ())), preferred_element_type=F32)
        r_ref[...] = gamc * r + kv
        y = inner + cross
        mu = jnp.mean(y, axis=-1, keepdims=True)
        yc = y - mu
        var = jnp.mean(yc * yc, axis=-1, keepdims=True)
        yn = yc * lax.rsqrt(var + EPS) * gn
        g = g_ref[rs, :].astype(F32)
        o_ref[rs, :] = (_silu(g) * yn).astype(o_ref.dtype)


def _rotary_tables(seq, dk):
    half = dk // 2
    pos = jnp.arange(seq, dtype=F32)
    freqs = ROPE_BASE ** (-jnp.arange(half, dtype=F32) / half)
    ang = pos[:, None] * freqs[None, :]
    cos, sin = jnp.cos(ang), jnp.sin(ang)
    kscale = dk ** -0.5
    return (jnp.stack([cos, cos * kscale, jnp.ones_like(cos)]),
            jnp.stack([sin, sin * kscale, jnp.zeros_like(sin)]))


def _retention_tables():
    h, c = N_HEADS_B, RET_CHUNK
    log_gamma = jnp.log1p(-jnp.exp2(-5.0 - jnp.arange(h, dtype=F32)))
    idx = jnp.arange(c, dtype=F32)
    rel = idx[:, None] - idx[None, :]
    decay = jnp.where(rel >= 0, jnp.exp(log_gamma[:, None, None] * jnp.maximum(rel, 0.0)), 0.0)
    xi = jnp.exp(log_gamma[:, None] * (idx + 1.0))[:, :, None]
    zeta = jnp.exp(log_gamma[:, None] * (c - 1.0 - idx))[:, :, None]
    gamc = jnp.exp(log_gamma * c)[:, None, None]
    return decay, xi, zeta, gamc


def _retention(proj, gn_g):
    b, s, n = proj.shape
    h = N_HEADS_B
    dv = gn_g.shape[-1] // h
    dk = (n - 2 * h * dv) // (2 * h)
    tc = RET_ROW_TILE
    c = RET_CHUNK
    decay, xi, zeta, gamc = _retention_tables()
    v_blk0 = (2 * h * dk) // dv
    out = pl.pallas_call(
        functools.partial(_retention_kernel, tc=tc),
        out_shape=jax.ShapeDtypeStruct((b, s, h * dv), BF16),
        grid=(b, h, s // tc),
        in_specs=[
            pl.BlockSpec((None, tc, dk), lambda bi, hi, t: (bi, t, hi)),
            pl.BlockSpec((None, tc, dk), lambda bi, hi, t: (bi, t, h + hi)),
            pl.BlockSpec((None, tc, dv), lambda bi, hi, t: (bi, t, v_blk0 + hi)),
            pl.BlockSpec((None, tc, dv), lambda bi, hi, t: (bi, t, v_blk0 + h + hi)),
            pl.BlockSpec((None, c, c), lambda bi, hi, t: (hi, 0, 0)),
            pl.BlockSpec((None, c, 1), lambda bi, hi, t: (hi, 0, 0)),
            pl.BlockSpec((None, c, 1), lambda bi, hi, t: (hi, 0, 0)),
            pl.BlockSpec((None, 1, 1), lambda bi, hi, t: (hi, 0, 0)),
            pl.BlockSpec((1, dv), lambda bi, hi, t: (0, hi)),
        ],
        out_specs=pl.BlockSpec((None, tc, dv), lambda bi, hi, t: (bi, t, hi)),
        scratch_shapes=[pltpu.VMEM((dk, dv), F32)],
        compiler_params=_params(3),
        name="retention",
    )(proj, proj, proj, proj, decay, xi, zeta, gamc, gn_g.reshape(1, h * dv))
    return out.reshape(b * s, h * dv)


def _ffn_kernel(x_ref, gain_ref, shift_ref, scale_ref, gate_ref, wa_ref, wb_ref,
                conv_ref, wd_ref, fg_ref, o_ref, h_ref,
                ua_ref, ub_ref, carry_ref, *, tiles_per_seq, final_norm):
    i = pl.program_id(0)
    f = pl.program_id(1)
    pad = CONV_HALO_ROWS
    tm = x_ref.shape[0]
    u_refs = (ua_ref, ub_ref)

    @pl.when(f == 0)
    def _():
        _norm_mod_rows(x_ref, gain_ref, shift_ref, scale_ref, h_ref)
        o_ref[...] = jnp.zeros_like(o_ref)

    @pl.when(i % tiles_per_seq == 0)
    def _():
        for u_ref in u_refs:
            u_ref[0:pad, :] = jnp.zeros((pad, u_ref.shape[1]), F32)

    @pl.when(i % tiles_per_seq != 0)
    def _():
        for br, u_ref in enumerate(u_refs):
            u_ref[0:pad, :] = carry_ref[f, br]

    def up_project(r0, r1):
        hr = h_ref[r0:r1, :]
        for u_ref, w_ref in zip(u_refs, (wa_ref, wb_ref)):
            u_ref[pad + r0:pad + r1, :] = jnp.dot(hr, w_ref[...], preferred_element_type=F32)

    nf = pl.num_programs(1)
    conv_params = (conv_ref[f], conv_ref[nf + f])

    def conv(u_ref, cp, r0, r1):
        y = cp[2:3, :] * u_ref[pad + r0:pad + r1, :]
        y = y + cp[1:2, :] * u_ref[pad - 1 + r0:pad - 1 + r1, :]
        y = y + cp[0:1, :] * u_ref[pad - 2 + r0:pad - 2 + r1, :]
        return y + cp[CONV_WIDTH:CONV_WIDTH + 1, :]

    def down_project(r0, r1):
        a = conv(ua_ref, conv_params[0], r0, r1)
        b = conv(ub_ref, conv_params[1], r0, r1)
        act = (_silu(a) * b).astype(BF16)
        o_ref[r0:r1, :] += jnp.dot(act, wd_ref[...], preferred_element_type=F32)

    bounds = list(range(0, tm + 1, FFN_ROW_CHUNK))
    chunks = list(zip(bounds[:-1], bounds[1:]))
    up_project(*chunks[0])
    for ci, (r0, r1) in enumerate(chunks):
        if ci + 1 < len(chunks):
            up_project(*chunks[ci + 1])
        down_project(r0, r1)

    for br, u_ref in enumerate(u_refs):
        carry_ref[f, br] = u_ref[tm:tm + pad, :]

    @pl.when(f == pl.num_programs(1) - 1)
    def _():
        xn = x_ref[...] + gate_ref[...] * o_ref[...]
        if final_norm:
            xn = _rmsnorm(xn, fg_ref[...])
        o_ref[...] = xn


def _pack_conv_params(conv_w, conv_b, tf):
    depth = conv_w.shape[0]
    blocks = conv_w.shape[-1] // tf
    rows = jnp.concatenate([conv_w.reshape(depth, CONV_WIDTH, blocks, tf),
                            conv_b.reshape(depth, 1, blocks, tf)], axis=1)
    rows = jnp.pad(rows, ((0, 0), (0, CONV_HALO_ROWS - CONV_WIDTH - 1), (0, 0), (0, 0)))
    return rows.transpose(0, 2, 1, 3)


def _conv_ffn(x2, seq, gain, shift, scale, gate, w_up, conv_pack, w_down, final_gain,
              layer, final_norm):
    m, d = x2.shape
    ff = w_down.shape[1]
    tm, tf = FFN_ROW_TILE, FFN_COL_TILE
    nf = ff // tf
    tiles_per_seq = seq // tm
    batch = lambda i, f: (i // tiles_per_seq, 0, 0)
    const = lambda i, f: (0, 0)
    return pl.pallas_call(
        functools.partial(_ffn_kernel, tiles_per_seq=tiles_per_seq, final_norm=final_norm),
        out_shape=jax.ShapeDtypeStruct((m, d), F32),
        grid=(m // tm, nf),
        in_specs=[
            pl.BlockSpec((tm, d), lambda i, f: (i, 0), pipeline_mode=pl.Buffered(1)),
            pl.BlockSpec((1, d), const),
            pl.BlockSpec((None, 1, d), batch),
            pl.BlockSpec((None, 1, d), batch),
            pl.BlockSpec((None, 1, d), batch),
            pl.BlockSpec((None, d, tf), lambda i, f: (layer, 0, f)),
            pl.BlockSpec((None, d, tf), lambda i, f: (layer, 0, nf + f)),
            pl.BlockSpec((None,) + conv_pack.shape[1:], lambda i, f: (layer, 0, 0, 0)),
            pl.BlockSpec((None, tf, d), lambda i, f: (layer, f, 0)),
            pl.BlockSpec((1, d), const),
        ],
        out_specs=pl.BlockSpec((tm, d), lambda i, f: (i, 0)),
        scratch_shapes=[pltpu.VMEM((tm, d), BF16),
                        pltpu.VMEM((tm + CONV_HALO_ROWS, tf), F32),
                        pltpu.VMEM((tm + CONV_HALO_ROWS, tf), F32),
                        pltpu.VMEM((nf, 2, CONV_HALO_ROWS, tf), F32)],
        compiler_params=_params(2),
        name="conv_ffn",
    )(x2, gain, shift, scale, gate, w_up, w_up, conv_pack, w_down, final_gain)


def kernel(x, c, mod_w, mod_b, norm_mix, w_in_a, w_out_a, w_in_b, gn_b, w_out_b, norm_ffn,
           ffn_up, ffn_conv_w, ffn_conv_b, ffn_down, final_norm):
    b, s, d = x.shape
    depth = mod_w.shape[0]
    assert depth == 2, "layer 0 = dilated attention mixer, layer 1 = retention mixer"
    mod = _modulation(c, mod_w, mod_b).reshape(depth, b, 6, 1, d)
    x2 = x.reshape(b * s, d)
    dilations = tuple(dl for _, dl in DILATED_GROUPS)
    conv_pack = _pack_conv_params(ffn_conv_w, ffn_conv_b, FFN_COL_TILE)
    later = (w_out_a, w_in_b, w_out_b, ffn_up, ffn_down)
    later_2d = tuple(a.reshape(-1, a.shape[-1]) for a in later)

    for i in range(depth):
        sh_a, sc_a, g_a, sh_m, sc_m, g_m = (mod[i, :, k] for k in range(6))
        gain = norm_mix[i].reshape(1, d)
        j = i // 2
        if i == 0:
            qkv, *cast = _project(x2, s, gain, sh_a, sc_a, w_in_a[j].astype(BF16), dilations,
                                  casts=later_2d)
            w_out_a_bf, w_in_b_bf, w_out_b_bf, ffn_up_bf, ffn_down_bf = (
                cb.reshape(a.shape) for cb, a in zip(cast, later))
            outs, lses = [], []
            for g, dil in enumerate(dilations):
                o, lse = _dilated_attention(qkv, s, g, dil)
                outs.append(o)
                lses.append(lse)
            x2 = _attn_out_project(x2, s, g_a, w_out_a_bf[j], outs, lses, dilations)
        else:
            dv = gn_b.shape[-1] // N_HEADS_B
            qk_cols = (w_in_b.shape[-1] - 2 * N_HEADS_B * dv) // 2
            cos, sin = _rotary_tables(s, qk_cols // N_HEADS_B)
            (proj,) = _project(x2, s, gain, sh_a, sc_a, w_in_b_bf[j], (1,),
                               rotary=(cos, sin, qk_cols))
            a = _retention(proj.reshape(b, s, -1), gn_b[j])
            x2 = _out_project(x2, s, g_a, w_out_b_bf[j], a)
        x2 = _conv_ffn(
            x2, s, norm_ffn[i].reshape(1, d), sh_m, sc_m, g_m, ffn_up_bf, conv_pack,
            ffn_down_bf, final_norm.reshape(1, d), layer=i, final_norm=(i == depth - 1))
    return x2.reshape(b, s, d)
```

```python
import functools

import jax
import jax.numpy as jnp
import numpy as np
from jax import lax
from jax.experimental import pallas as pl
from jax.experimental.pallas import tpu as pltpu

F32 = jnp.float32
BF16 = jnp.bfloat16

EPS = 1e-6
LOG2_E = 1.4426950408889634
HEAD_DIM_A = 128
N_HEADS_A = 16
WIDTH_A = N_HEADS_A * HEAD_DIM_A
DILATED_GROUPS = ((128, 1), (512, 4), (2048, 16))
ATTN_STEPS = 128
N_HEADS_B = 8
RET_CHUNK = 256
ROPE_BASE = 10000.0
CONV_WIDTH = 3

VMEM_LIMIT_BYTES = 60 * 1024 * 1024
LANES = 128
CONV_HALO_ROWS = 8
CAST_ROW_ALIGN = 16

PERM_TILE = 256
ATTN_OUT_ROW_TILE = 512
PROJ_ROW_TILE = 1024
PROJ_COL_TILES = (1024, 2048)
PROJ_VMEM_BUDGET_BYTES = 54 * 1024 * 1024
NORM_ROW_CHUNK = 16
NORM_UNROLL = 16
OUT_ROW_TILE = 1024
OUT_COL_TILE = 1024
FFN_ROW_TILE = 1024
FFN_COL_TILE = 512
FFN_ROW_CHUNK = 512
ATTN_ROW_TILE = 1024
ATTN_HEADS_PER_STEP = {1024: 4, 512: 16}
RET_ROW_TILE = 2048
MOD_COL_TILE = 1024


def _params(n_axes):
    return pltpu.CompilerParams(
        dimension_semantics=("arbitrary",) * n_axes,
        vmem_limit_bytes=VMEM_LIMIT_BYTES)


def _silu(x):
    h = 0.5 * x
    return h + h * jnp.tanh(h)


def _rmsnorm(x, gain):
    ms = jnp.mean(x * x, axis=-1, keepdims=True)
    return x * lax.rsqrt(ms + EPS) * gain


def _norm_mod_rows(x_ref, gain_ref, shift_ref, scale_ref, h_ref):
    shift = shift_ref[...]
    gain_mod = gain_ref[...] * (1.0 + scale_ref[...])
    rows = NORM_ROW_CHUNK

    def body(k, carry):
        rs = pl.ds(pl.multiple_of(k * rows, rows), rows)
        h_ref[rs, :] = (_rmsnorm(x_ref[rs, :], gain_mod) + shift).astype(h_ref.dtype)
        return carry

    lax.fori_loop(0, x_ref.shape[0] // rows, body, 0, unroll=NORM_UNROLL)


def _class_major_perm(dilation):
    c = PERM_TILE // dilation
    p = np.arange(PERM_TILE)
    m = np.zeros((PERM_TILE, PERM_TILE), np.float32)
    m[p, (p % c) * dilation + p // c] = 1.0
    return m


def _mod_kernel(c_ref, w_ref, b_ref, o_ref):
    c = c_ref[...]
    c_act = (c * jax.nn.sigmoid(c)).astype(BF16)
    o_ref[...] = jnp.dot(c_act, w_ref[...].astype(BF16),
                         preferred_element_type=F32) + b_ref[...]


def _modulation(c, mod_w, mod_b):
    depth, d, n = mod_w.shape
    b = c.shape[0]
    tn = MOD_COL_TILE
    return pl.pallas_call(
        _mod_kernel,
        out_shape=jax.ShapeDtypeStruct((depth, b, n), F32),
        grid=(depth, n // tn),
        in_specs=[
            pl.BlockSpec((b, d), lambda l, j: (0, 0)),
            pl.BlockSpec((None, d, tn), lambda l, j: (l, 0, j)),
            pl.BlockSpec((None, 1, tn), lambda l, j: (l, 0, j)),
        ],
        out_specs=pl.BlockSpec((None, b, tn), lambda l, j: (l, 0, j)),
        compiler_params=_params(2),
        name="modulation",
    )(c, mod_w, mod_b.reshape(depth, 1, n))


def _proj_kernel(*refs, n_groups, col_blocks_per_group, rotary, n_casts):
    x_ref, gain_ref, shift_ref, scale_ref = refs[:4]
    rest = list(refs[4:])
    perm_ref = rest.pop(0) if n_groups > 1 else None
    cos_ref, sin_ref = (rest.pop(0), rest.pop(0)) if rotary else (None, None)
    w_ref = rest.pop(0)
    cast_src = [rest.pop(0) for _ in range(n_casts)]
    o_ref = rest.pop(0)
    cast_dst = [rest.pop(0) for _ in range(n_casts)]
    (h_ref,) = rest
    j = pl.program_id(1)
    tm = x_ref.shape[0]

    @pl.when(j == 0)
    def _():
        _norm_mod_rows(x_ref, gain_ref, shift_ref, scale_ref, h_ref.at[0])
        for g in range(1, n_groups):
            for t0 in range(0, tm, PERM_TILE):
                rs = slice(t0, t0 + PERM_TILE)
                h_ref[g, rs, :] = jnp.dot(perm_ref[g - 1], h_ref[0, rs, :],
                                          preferred_element_type=F32).astype(BF16)

    g = j // col_blocks_per_group
    y = jnp.dot(h_ref[g], w_ref[...], preferred_element_type=F32)
    if rotary:
        cos, sin = cos_ref[...], sin_ref[...]
        half = cos.shape[1]
        for c0 in range(0, y.shape[1], 2 * half):
            y1, y2 = y[:, c0:c0 + half], y[:, c0 + half:c0 + 2 * half]
            o_ref[:, c0:c0 + half] = (y1 * cos - y2 * sin).astype(o_ref.dtype)
            o_ref[:, c0 + half:c0 + 2 * half] = (y1 * sin + y2 * cos).astype(o_ref.dtype)
    else:
        o_ref[...] = y.astype(o_ref.dtype)

    for src, dst in zip(cast_src, cast_dst):
        dst[...] = src[...].astype(dst.dtype)


def _cast_row_block(n_rows, n_steps):
    br = CAST_ROW_ALIGN
    while n_rows % br or n_rows // br > n_steps:
        br *= 2
    return br


def _project(x2, seq, gain, shift, scale, w, dilations, rotary=None, casts=()):
    m, d = x2.shape
    n = w.shape[1]
    tm = PROJ_ROW_TILE
    n_groups = len(dilations)

    def vmem_bytes(tn):
        steps = (m // tm) * (n // tn)
        cast = sum(2 * _cast_row_block(a.shape[0], steps) * a.shape[1] * (4 + 2) for a in casts)
        return (2 * (tm * d * 4 + d * tn * 2 + tm * tn * 2) + n_groups * tm * d * 2
                + tm * tn * 4 + cast)

    tn = max(t for t in PROJ_COL_TILES if vmem_bytes(t) <= PROJ_VMEM_BUDGET_BYTES)
    tiles_per_seq = seq // tm
    batch = lambda i, j: (i // tiles_per_seq, 0, 0)
    in_specs = [
        pl.BlockSpec((tm, d), lambda i, j: (i, 0)),
        pl.BlockSpec((1, d), lambda i, j: (0, 0)),
        pl.BlockSpec((None, 1, d), batch),
        pl.BlockSpec((None, 1, d), batch),
    ]
    args = [x2, gain, shift, scale]
    if n_groups > 1:
        assert dilations[0] == 1
        perms = jnp.asarray(np.stack([_class_major_perm(dl) for dl in dilations[1:]]), BF16)
        in_specs.append(pl.BlockSpec(perms.shape, lambda i, j: (0, 0, 0)))
        args.append(perms)
    if rotary is not None:
        cos, sin, cols_per_kind = rotary
        blocks_per_kind = cols_per_kind // tn
        last_kind = cos.shape[0] - 1
        table = lambda i, j: (jnp.minimum(j // blocks_per_kind, last_kind),
                              i % tiles_per_seq, 0)
        in_specs += [pl.BlockSpec((None, tm, cos.shape[2]), table)] * 2
        args += [cos, sin]
    in_specs.append(pl.BlockSpec((d, tn), lambda i, j: (0, j)))
    args.append(w)
    nj = n // tn
    out_specs = [pl.BlockSpec((tm, tn), lambda i, j: (i, j))]
    out_shapes = [jax.ShapeDtypeStruct((m, n), BF16)]
    for a in casts:
        br = _cast_row_block(a.shape[0], (m // tm) * nj)
        last = a.shape[0] // br - 1
        spec = pl.BlockSpec((br, a.shape[1]),
                            lambda i, j, last=last: (jnp.minimum(i * nj + j, last), 0))
        in_specs.append(spec)
        args.append(a)
        out_specs.append(spec)
        out_shapes.append(jax.ShapeDtypeStruct(a.shape, BF16))
    return pl.pallas_call(
        functools.partial(_proj_kernel, n_groups=n_groups,
                          col_blocks_per_group=n // n_groups // tn,
                          rotary=rotary is not None, n_casts=len(casts)),
        out_shape=out_shapes,
        grid=(m // tm, nj),
        in_specs=in_specs,
        out_specs=out_specs,
        scratch_shapes=[pltpu.VMEM((n_groups, tm, d), BF16)],
        compiler_params=_params(2),
        name="in_project",
    )(*args)


def _row_pieces(c, r0, n):
    pieces = []
    r = r0
    while r < r0 + n:
        stop = min((r // c + 1) * c, r0 + n)
        pieces.append((r // c, r % c, stop - (r // c) * c))
        r = stop
    return pieces


def _load_rows(ref, r0, n, cs):
    parts = [ref[k, a:b, cs] for k, a, b in _row_pieces(ref.shape[1], r0, n)]
    return parts[0] if len(parts) == 1 else jnp.concatenate(parts, axis=0)


def _store_rows(ref, r0, cs, val):
    off = 0
    for k, a, b in _row_pieces(ref.shape[1], r0, val.shape[0]):
        ref[k, a:b, cs] = val[off:off + b - a]
        off += b - a


def _attn_kernel(q_ref, k_ref, v_ref, kp_ref, vp_ref, o_ref, lse_ref, *, tq, hb):
    n = pl.program_id(2)
    hblk = pl.program_id(3)
    w = ATTN_STEPS
    nq = tq // w
    scale = HEAD_DIM_A ** -0.5
    all_lanes = slice(0, LANES)

    @pl.when(hblk == 0)
    def _():
        lse_ref[...] = jnp.zeros_like(lse_ref)

    qi = lax.broadcasted_iota(jnp.int32, (w, 2 * w), 0)
    kj = lax.broadcasted_iota(jnp.int32, (w, 2 * w), 1)
    band = (kj >= qi) & (kj <= qi + w)
    band_first = band & ((kj >= w) | (n > 0))
    lane = lax.broadcasted_iota(jnp.int32, (w, LANES), 1)

    for hh in range(hb):
        cs = slice(hh * HEAD_DIM_A, (hh + 1) * HEAD_DIM_A)
        head = hblk * hb + hh
        for qb in range(nq):
            q = _load_rows(q_ref, qb * w, w, cs)
            if qb == 0:
                kcat = jnp.concatenate([_load_rows(kp_ref, 0, w, cs),
                                        _load_rows(k_ref, 0, w, cs)], axis=0)
                vcat = jnp.concatenate([_load_rows(vp_ref, 0, w, cs),
                                        _load_rows(v_ref, 0, w, cs)], axis=0)
                mask = band_first
            else:
                kcat = _load_rows(k_ref, (qb - 1) * w, 2 * w, cs)
                vcat = _load_rows(v_ref, (qb - 1) * w, 2 * w, cs)
                mask = band
            s = lax.dot_general(q, kcat, (((1,), (1,)), ((), ())),
                                preferred_element_type=F32)
            s = jnp.where(mask, s, -jnp.inf)
            m = jnp.max(s, axis=-1, keepdims=True)
            p = jnp.exp2((s - m) * (scale * LOG2_E))
            den = jnp.sum(p, axis=-1, keepdims=True)
            o = jnp.dot(p.astype(BF16), vcat, preferred_element_type=F32) / den
            _store_rows(o_ref, qb * w, cs, o.astype(o_ref.dtype))
            lse = m * scale + jnp.log(den)
            prev = _load_rows(lse_ref, qb * w, w, all_lanes)
            _store_rows(lse_ref, qb * w, all_lanes, jnp.where(lane == head, lse, prev))


def _dilated_attention(qkv, seq, group, dilation):
    m, n_all = qkv.shape
    b = m // seq
    d = dilation
    c = PERM_TILE // d
    tiles = seq // PERM_TILE
    tq = min(ATTN_ROW_TILE, seq // d)
    tpq = tq // c
    hb = ATTN_HEADS_PER_STEP[tq]
    cw = hb * HEAD_DIM_A
    ncb = WIDTH_A // cw
    w = ATTN_STEPS
    col0 = group * 3 * ncb
    pc = min(c, w)
    ptiles = w // pc
    rows_view = lambda a: a.reshape(b, tiles, d, c, a.shape[-1])

    def prev_map(section):
        if c >= w:
            return lambda bi, r, nn, h: (bi, jnp.maximum(nn * tpq - 1, 0), r, c // w - 1,
                                         col0 + section * ncb + h)
        return lambda bi, r, nn, h: (bi, jnp.maximum(nn * (tpq // ptiles) - 1, 0), r, 0,
                                     col0 + section * ncb + h)

    def cur_map(section):
        return lambda bi, r, nn, h: (bi, nn, r, 0, col0 + section * ncb + h)

    qv = rows_view(qkv)
    o, lse = pl.pallas_call(
        functools.partial(_attn_kernel, tq=tq, hb=hb),
        out_shape=(jax.ShapeDtypeStruct((b, tiles, d, c, WIDTH_A), BF16),
                   jax.ShapeDtypeStruct((b, tiles, d, c, LANES), F32)),
        grid=(b, d, seq // d // tq, ncb),
        in_specs=[
            pl.BlockSpec((None, tpq, None, c, cw), cur_map(0)),
            pl.BlockSpec((None, tpq, None, c, cw), cur_map(1)),
            pl.BlockSpec((None, tpq, None, c, cw), cur_map(2)),
            pl.BlockSpec((None, ptiles, None, pc, cw), prev_map(1)),
            pl.BlockSpec((None, ptiles, None, pc, cw), prev_map(2)),
        ],
        out_specs=(
            pl.BlockSpec((None, tpq, None, c, cw), lambda bi, r, nn, h: (bi, nn, r, 0, h)),
            pl.BlockSpec((None, tpq, None, c, LANES), lambda bi, r, nn, h: (bi, nn, r, 0, 0)),
        ),
        compiler_params=_params(4),
        name=f"dilated_attention_d{d}",
    )(qv, qv, qv, qv, qv)
    return o.reshape(m, WIDTH_A), lse.reshape(m, LANES)


def _merge_tile(o_refs, lse_refs, pt_ref, lsn_refs, a_ref, dilations):
    perm_tiles = range(0, a_ref.shape[0], PERM_TILE)
    ls = [lse_refs[0][...]]
    for g in range(1, len(dilations)):
        d = dilations[g]
        c = PERM_TILE // d
        for t0 in perm_tiles:
            for r in range(d):
                lsn_refs[g - 1][pl.ds(t0 + r, c, stride=d), :] = (
                    lse_refs[g][t0 + r * c:t0 + (r + 1) * c, :])
        ls.append(lsn_refs[g - 1][...])
    mx = functools.reduce(jnp.maximum, ls)
    es = [jnp.exp(l - mx) for l in ls]
    inv = 1.0 / functools.reduce(lambda u, v: u + v, es)
    alphas = [e * inv for e in es[1:]]
    heads_per_dot = 2
    for hp in range(N_HEADS_A // heads_per_dot):
        cs2 = slice(hp * heads_per_dot * HEAD_DIM_A, (hp + 1) * heads_per_dot * HEAD_DIM_A)
        base = o_refs[0][:, cs2].astype(F32)
        others = [jnp.concatenate(
            [jnp.dot(pt_ref[g - 1], o_refs[g][t0:t0 + PERM_TILE, cs2],
                     preferred_element_type=F32) for t0 in perm_tiles], axis=0)
            for g in range(1, len(dilations))]
        for hh in range(heads_per_dot):
            h = hp * heads_per_dot + hh
            ls_ = slice(hh * HEAD_DIM_A, (hh + 1) * HEAD_DIM_A)
            acc = base[:, ls_]
            for al, og in zip(alphas, others):
                acc = acc + al[:, h:h + 1] * (og[:, ls_] - base[:, ls_])
            a_ref[:, h * HEAD_DIM_A:(h + 1) * HEAD_DIM_A] = acc.astype(a_ref.dtype)


def _attn_out_kernel(o0, o1, o2, l0, l1, l2, pt_ref, w_ref, x_ref, gate_ref, out_ref,
                     a0_ref, a1_ref, lsn1_ref, lsn2_ref, *, dilations):
    s = pl.program_id(0)

    @pl.when(s == 0)
    def _():
        a1_ref[...] = jnp.zeros_like(a1_ref)

    def step(a_new, a_old):
        _merge_tile((o0, o1, o2), (l0, l1, l2), pt_ref, (lsn1_ref, lsn2_ref), a_new, dilations)
        y = jnp.dot(a_old[...], w_ref[...], preferred_element_type=F32)
        out_ref[...] = x_ref[...] + gate_ref[...] * y

    @pl.when(s % 2 == 0)
    def _():
        step(a0_ref, a1_ref)

    @pl.when(s % 2 == 1)
    def _():
        step(a1_ref, a0_ref)


def _attn_out_project(x2, seq, gate, w, outs, lses, dilations):
    m, d = x2.shape
    k = w.shape[0]
    tm = ATTN_OUT_ROW_TILE
    n_tiles = m // tm
    tiles_per_seq = seq // tm
    pts = jnp.asarray(np.stack([_class_major_perm(dl).T for dl in dilations[1:]]), BF16)
    cur = lambda s: (jnp.minimum(s, n_tiles - 1), 0)
    prev = lambda s: (jnp.maximum(s - 1, 0), 0)
    in_specs = ([pl.BlockSpec((tm, k), cur) for _ in outs]
                + [pl.BlockSpec((tm, LANES), cur) for _ in lses]
                + [pl.BlockSpec(pts.shape, lambda s: (0, 0, 0)),
                   pl.BlockSpec((k, d), lambda s: (0, 0), pipeline_mode=pl.Buffered(1)),
                   pl.BlockSpec((tm, d), prev),
                   pl.BlockSpec((None, 1, d),
                                lambda s: (jnp.maximum(s - 1, 0) // tiles_per_seq, 0, 0))])
    return pl.pallas_call(
        functools.partial(_attn_out_kernel, dilations=dilations),
        out_shape=jax.ShapeDtypeStruct((m, d), F32),
        grid=(n_tiles + 1,),
        in_specs=in_specs,
        out_specs=pl.BlockSpec((tm, d), prev),
        scratch_shapes=[pltpu.VMEM((tm, k), BF16), pltpu.VMEM((tm, k), BF16),
                        pltpu.VMEM((tm, LANES), F32), pltpu.VMEM((tm, LANES), F32)],
        compiler_params=_params(1),
        name="attn_out_project",
    )(*outs, *lses, pts, w, x2, gate)


def _out_kernel(a_ref, w_ref, x_ref, gate_ref, out_ref):
    y = jnp.dot(a_ref[...], w_ref[...], preferred_element_type=F32)
    out_ref[...] = x_ref[...] + gate_ref[...] * y


def _out_project(x2, seq, gate, w, a):
    m, d = x2.shape
    k = w.shape[0]
    tm, tn = OUT_ROW_TILE, OUT_COL_TILE
    tiles_per_seq = seq // tm
    return pl.pallas_call(
        _out_kernel,
        out_shape=jax.ShapeDtypeStruct((m, d), F32),
        grid=(m // tm, d // tn),
        in_specs=[
            pl.BlockSpec((tm, k), lambda i, j: (i, 0)),
            pl.BlockSpec((k, tn), lambda i, j: (0, j)),
            pl.BlockSpec((tm, tn), lambda i, j: (i, j)),
            pl.BlockSpec((None, 1, tn), lambda i, j: (i // tiles_per_seq, 0, j)),
        ],
        out_specs=pl.BlockSpec((tm, tn), lambda i, j: (i, j)),
        compiler_params=_params(2),
        name="out_project",
    )(a, w, x2, gate)


def _retention_kernel(q_ref, k_ref, v_ref, g_ref, decay_ref, xi_ref, zeta_ref, gamc_ref,
                      gn_ref, o_ref, r_ref, *, tc):
    c = RET_CHUNK

    @pl.when(pl.program_id(2) == 0)
    def _():
        r_ref[...] = jnp.zeros_like(r_ref)

    decay = decay_ref[...]
    xi = xi_ref[...]
    zeta = zeta_ref[...]
    gamc = gamc_ref[...]
    gn = gn_ref[...]

    for ci in range(tc // c):
        rs = slice(ci * c, (ci + 1) * c)
        q = q_ref[rs, :]
        k = k_ref[rs, :]
        v = v_ref[rs, :]
        s = lax.dot_general(q, k, (((1,), (1,)), ((), ())), preferred_element_type=F32) * decay
        inner = jnp.dot(s.astype(BF16), v, preferred_element_type=F32)
        r = r_ref[...]
        cross = jnp.dot(q, r.astype(BF16), preferred_element_type=F32) * xi
        kz = (k.astype(F32) * zeta).astype(BF16)
        kv = lax.dot_general(kz, v, (((0,), (0,)), ((), ())), preferred_element_type=F32)
        r_ref[...] = gamc * r + kv
        y = inner + cross
        mu = jnp.mean(y, axis=-1, keepdims=True)
        yc = y - mu
        var = jnp.mean(yc * yc, axis=-1, keepdims=True)
        yn = yc * lax.rsqrt(var + EPS) * gn
        g = g_ref[rs, :].astype(F32)
        o_ref[rs, :] = (_silu(g) * yn).astype(o_ref.dtype)


def _rotary_tables(seq, dk):
    half = dk // 2
    pos = jnp.arange(seq, dtype=F32)
    freqs = ROPE_BASE ** (-jnp.arange(half, dtype=F32) / half)
    ang = pos[:, None] * freqs[None, :]
    cos, sin = jnp.cos(ang), jnp.sin(ang)
    kscale = dk ** -0.5
    return (jnp.stack([cos, cos * kscale, jnp.ones_like(cos)]),
            jnp.stack([sin, sin * kscale, jnp.zeros_like(sin)]))


def _retention_tables():
    h, c = N_HEADS_B, RET_CHUNK
    log_gamma = jnp.log1p(-jnp.exp2(-5.0 - jnp.arange(h, dtype=F32)))
    idx = jnp.arange(c, dtype=F32)
    rel = idx[:, None] - idx[None, :]
    decay = jnp.where(rel >= 0, jnp.exp(log_gamma[:, None, None] * jnp.maximum(rel, 0.0)), 0.0)
    xi = jnp.exp(log_gamma[:, None] * (idx + 1.0))[:, :, None]
    zeta = jnp.exp(log_gamma[:, None] * (c - 1.0 - idx))[:, :, None]
    gamc = jnp.exp(log_gamma * c)[:, None, None]
    return decay, xi, zeta, gamc


def _retention(proj, gn_g):
    b, s, n = proj.shape
    h = N_HEADS_B
    dv = gn_g.shape[-1] // h
    dk = (n - 2 * h * dv) // (2 * h)
    tc = RET_ROW_TILE
    c = RET_CHUNK
    decay, xi, zeta, gamc = _retention_tables()
    v_blk0 = (2 * h * dk) // dv
    out = pl.pallas_call(
        functools.partial(_retention_kernel, tc=tc),
        out_shape=jax.ShapeDtypeStruct((b, s, h * dv), BF16),
        grid=(b, h, s // tc),
        in_specs=[
            pl.BlockSpec((None, tc, dk), lambda bi, hi, t: (bi, t, hi)),
            pl.BlockSpec((None, tc, dk), lambda bi, hi, t: (bi, t, h + hi)),
            pl.BlockSpec((None, tc, dv), lambda bi, hi, t: (bi, t, v_blk0 + hi)),
            pl.BlockSpec((None, tc, dv), lambda bi, hi, t: (bi, t, v_blk0 + h + hi)),
            pl.BlockSpec((None, c, c), lambda bi, hi, t: (hi, 0, 0)),
            pl.BlockSpec((None, c, 1), lambda bi, hi, t: (hi, 0, 0)),
            pl.BlockSpec((None, c, 1), lambda bi, hi, t: (hi, 0, 0)),
            pl.BlockSpec((None, 1, 1), lambda bi, hi, t: (hi, 0, 0)),
            pl.BlockSpec((1, dv), lambda bi, hi, t: (0, hi)),
        ],
        out_specs=pl.BlockSpec((None, tc, dv), lambda bi, hi, t: (bi, t, hi)),
        scratch_shapes=[pltpu.VMEM((dk, dv), F32)],
        compiler_params=_params(3),
        name="retention",
    )(proj, proj, proj, proj, decay, xi, zeta, gamc, gn_g.reshape(1, h * dv))
    return out.reshape(b * s, h * dv)


def _ffn_kernel(x_ref, gain_ref, shift_ref, scale_ref, gate_ref, wa_ref, wb_ref,
                conv_ref, wd_ref, fg_ref, o_ref, h_ref,
                ua_ref, ub_ref, carry_ref, *, tiles_per_seq, final_norm):
    i = pl.program_id(0)
    f = pl.program_id(1)
    pad = CONV_HALO_ROWS
    tm = x_ref.shape[0]
    u_refs = (ua_ref, ub_ref)

    @pl.when(f == 0)
    def _():
        _norm_mod_rows(x_ref, gain_ref, shift_ref, scale_ref, h_ref)
        o_ref[...] = jnp.zeros_like(o_ref)

    @pl.when(i % tiles_per_seq == 0)
    def _():
        for u_ref in u_refs:
            u_ref[0:pad, :] = jnp.zeros((pad, u_ref.shape[1]), F32)

    @pl.when(i % tiles_per_seq != 0)
    def _():
        for br, u_ref in enumerate(u_refs):
            u_ref[0:pad, :] = carry_ref[f, br]

    def up_project(r0, r1):
        hr = h_ref[r0:r1, :]
        for u_ref, w_ref in zip(u_refs, (wa_ref, wb_ref)):
            u_ref[pad + r0:pad + r1, :] = jnp.dot(hr, w_ref[...], preferred_element_type=F32)

    nf = pl.num_programs(1)
    conv_params = (conv_ref[f], conv_ref[nf + f])

    def conv(u_ref, cp, r0, r1):
        y = cp[2:3, :] * u_ref[pad + r0:pad + r1, :]
        y = y + cp[1:2, :] * u_ref[pad - 1 + r0:pad - 1 + r1, :]
        y = y + cp[0:1, :] * u_ref[pad - 2 + r0:pad - 2 + r1, :]
        return y + cp[CONV_WIDTH:CONV_WIDTH + 1, :]

    def down_project(r0, r1):
        a = conv(ua_ref, conv_params[0], r0, r1)
        b = conv(ub_ref, conv_params[1], r0, r1)
        act = (_silu(a) * b).astype(BF16)
        o_ref[r0:r1, :] += jnp.dot(act, wd_ref[...], preferred_element_type=F32)

    bounds = list(range(0, tm + 1, FFN_ROW_CHUNK))
    chunks = list(zip(bounds[:-1], bounds[1:]))
    up_project(*chunks[0])
    for ci, (r0, r1) in enumerate(chunks):
        if ci + 1 < len(chunks):
            up_project(*chunks[ci + 1])
        down_project(r0, r1)

    for br, u_ref in enumerate(u_refs):
        carry_ref[f, br] = u_ref[tm:tm + pad, :]

    @pl.when(f == pl.num_programs(1) - 1)
    def _():
        xn = x_ref[...] + gate_ref[...] * o_ref[...]
        if final_norm:
            xn = _rmsnorm(xn, fg_ref[...])
        o_ref[...] = xn


def _pack_conv_params(conv_w, conv_b, tf):
    depth = conv_w.shape[0]
    blocks = conv_w.shape[-1] // tf
    rows = jnp.concatenate([conv_w.reshape(depth, CONV_WIDTH, blocks, tf),
                            conv_b.reshape(depth, 1, blocks, tf)], axis=1)
    rows = jnp.pad(rows, ((0, 0), (0, CONV_HALO_ROWS - CONV_WIDTH - 1), (0, 0), (0, 0)))
    return rows.transpose(0, 2, 1, 3)


def _conv_ffn(x2, seq, gain, shift, scale, gate, w_up, conv_pack, w_down, final_gain,
              layer, final_norm):
    m, d = x2.shape
    ff = w_down.shape[1]
    tm, tf = FFN_ROW_TILE, FFN_COL_TILE
    nf = ff // tf
    tiles_per_seq = seq // tm
    batch = lambda i, f: (i // tiles_per_seq, 0, 0)
    const = lambda i, f: (0, 0)
    return pl.pallas_call(
        functools.partial(_ffn_kernel, tiles_per_seq=tiles_per_seq, final_norm=final_norm),
        out_shape=jax.ShapeDtypeStruct((m, d), F32),
        grid=(m // tm, nf),
        in_specs=[
            pl.BlockSpec((tm, d), lambda i, f: (i, 0), pipeline_mode=pl.Buffered(1)),
            pl.BlockSpec((1, d), const),
            pl.BlockSpec((None, 1, d), batch),
            pl.BlockSpec((None, 1, d), batch),
            pl.BlockSpec((None, 1, d), batch),
            pl.BlockSpec((None, d, tf), lambda i, f: (layer, 0, f)),
            pl.BlockSpec((None, d, tf), lambda i, f: (layer, 0, nf + f)),
            pl.BlockSpec((None,) + conv_pack.shape[1:], lambda i, f: (layer, 0, 0, 0)),
            pl.BlockSpec((None, tf, d), lambda i, f: (layer, f, 0)),
            pl.BlockSpec((1, d), const),
        ],
        out_specs=pl.BlockSpec((tm, d), lambda i, f: (i, 0)),
        scratch_shapes=[pltpu.VMEM((tm, d), BF16),
                        pltpu.VMEM((tm + CONV_HALO_ROWS, tf), F32),
                        pltpu.VMEM((tm + CONV_HALO_ROWS, tf), F32),
                        pltpu.VMEM((nf, 2, CONV_HALO_ROWS, tf), F32)],
        compiler_params=_params(2),
        name="conv_ffn",
    )(x2, gain, shift, scale, gate, w_up, w_up, conv_pack, w_down, final_gain)


def kernel(x, c, mod_w, mod_b, norm_mix, w_in_a, w_out_a, w_in_b, gn_b, w_out_b, norm_ffn,
           ffn_up, ffn_conv_w, ffn_conv_b, ffn_down, final_norm):
    b, s, d = x.shape
    depth = mod_w.shape[0]
    assert depth == 2, "layer 0 = dilated attention mixer, layer 1 = retention mixer"
    mod = _modulation(c, mod_w, mod_b).reshape(depth, b, 6, 1, d)
    x2 = x.reshape(b * s, d)
    dilations = tuple(dl for _, dl in DILATED_GROUPS)
    conv_pack = _pack_conv_params(ffn_conv_w, ffn_conv_b, FFN_COL_TILE)
    later = (w_out_a, w_in_b, w_out_b, ffn_up, ffn_down)
    later_2d = tuple(a.reshape(-1, a.shape[-1]) for a in later)

    for i in range(depth):
        sh_a, sc_a, g_a, sh_m, sc_m, g_m = (mod[i, :, k] for k in range(6))
        gain = norm_mix[i].reshape(1, d)
        j = i // 2
        if i == 0:
            qkv, *cast = _project(x2, s, gain, sh_a, sc_a, w_in_a[j].astype(BF16), dilations,
                                  casts=later_2d)
            w_out_a_bf, w_in_b_bf, w_out_b_bf, ffn_up_bf, ffn_down_bf = (
                cb.reshape(a.shape) for cb, a in zip(cast, later))
            outs, lses = [], []
            for g, dil in enumerate(dilations):
                o, lse = _dilated_attention(qkv, s, g, dil)
                outs.append(o)
                lses.append(lse)
            x2 = _attn_out_project(x2, s, g_a, w_out_a_bf[j], outs, lses, dilations)
        else:
            dv = gn_b.shape[-1] // N_HEADS_B
            qk_cols = (w_in_b.shape[-1] - 2 * N_HEADS_B * dv) // 2
            cos, sin = _rotary_tables(s, qk_cols // N_HEADS_B)
            (proj,) = _project(x2, s, gain, sh_a, sc_a, w_in_b_bf[j], (1,),
                               rotary=(cos, sin, qk_cols))
            a = _retention(proj.reshape(b, s, -1), gn_b[j])
            x2 = _out_project(x2, s, g_a, w_out_b_bf[j], a)
        x2 = _conv_ffn(
            x2, s, norm_ffn[i].reshape(1, d), sh_m, sc_m, g_m, ffn_up_bf, conv_pack,
            ffn_down_bf, final_norm.reshape(1, d), layer=i, final_norm=(i == depth - 1))
    return x2.reshape(b, s, d)
```

```python
import functools

import jax
import jax.numpy as jnp
import numpy as np
from jax import lax
from jax.experimental import pallas as pl
from jax.experimental.pallas import tpu as pltpu

F32 = jnp.float32
BF16 = jnp.bfloat16

EPS = 1e-6
LOG2_E = 1.4426950408889634
HEAD_DIM_A = 128
N_HEADS_A = 16
WIDTH_A = N_HEADS_A * HEAD_DIM_A
DILATED_GROUPS = ((128, 1), (512, 4), (2048, 16))
ATTN_STEPS = 128
N_HEADS_B = 8
RET_CHUNK = 256
ROPE_BASE = 10000.0
CONV_WIDTH = 3

VMEM_LIMIT_BYTES = 60 * 1024 * 1024
LANES = 128
CONV_HALO_ROWS = 8
CAST_ROW_ALIGN = 16

PERM_TILE = 256
ATTN_OUT_ROW_TILE = 512
PROJ_ROW_TILE = 1024
PROJ_COL_TILES = (1024, 2048)
PROJ_VMEM_BUDGET_BYTES = 54 * 1024 * 1024
NORM_ROW_CHUNK = 16
NORM_UNROLL = 16
OUT_ROW_TILE = 1024
OUT_COL_TILE = 1024
FFN_ROW_TILE = 1024
FFN_COL_TILE = 512
FFN_ROW_CHUNK = 512
ATTN_ROW_TILE = 1024
ATTN_HEADS_PER_STEP = {1024: 4, 512: 16}
RET_ROW_TILE = 4096
MOD_COL_TILE = 2048


def _params(n_axes):
    return pltpu.CompilerParams(
        dimension_semantics=("arbitrary",) * n_axes,
        vmem_limit_bytes=VMEM_LIMIT_BYTES)


def _silu(x):
    h = 0.5 * x
    return h + h * jnp.tanh(h)


def _rmsnorm(x, gain):
    ms = jnp.mean(x * x, axis=-1, keepdims=True)
    return x * lax.rsqrt(ms + EPS) * gain


def _norm_mod_rows(x_ref, gain_ref, shift_ref, scale_ref, h_ref):
    shift = shift_ref[...]
    gain_mod = gain_ref[...] * (1.0 + scale_ref[...])
    rows = NORM_ROW_CHUNK

    def body(k, carry):
        rs = pl.ds(pl.multiple_of(k * rows, rows), rows)
        h_ref[rs, :] = (_rmsnorm(x_ref[rs, :], gain_mod) + shift).astype(h_ref.dtype)
        return carry

    lax.fori_loop(0, x_ref.shape[0] // rows, body, 0, unroll=NORM_UNROLL)


def _class_major_perm(dilation):
    c = PERM_TILE // dilation
    p = np.arange(PERM_TILE)
    m = np.zeros((PERM_TILE, PERM_TILE), np.float32)
    m[p, (p % c) * dilation + p // c] = 1.0
    return m


def _mod_kernel(c_ref, w_ref, b_ref, o_ref):
    c = c_ref[...]
    c_act = (c * jax.nn.sigmoid(c)).astype(BF16)
    o_ref[...] = jnp.dot(c_act, w_ref[...].astype(BF16),
                         preferred_element_type=F32) + b_ref[...]


def _modulation(c, mod_w, mod_b):
    depth, d, n = mod_w.shape
    b = c.shape[0]
    tn = MOD_COL_TILE
    return pl.pallas_call(
        _mod_kernel,
        out_shape=jax.ShapeDtypeStruct((depth, b, n), F32),
        grid=(depth, n // tn),
        in_specs=[
            pl.BlockSpec((b, d), lambda l, j: (0, 0)),
            pl.BlockSpec((None, d, tn), lambda l, j: (l, 0, j)),
            pl.BlockSpec((None, 1, tn), lambda l, j: (l, 0, j)),
        ],
        out_specs=pl.BlockSpec((None, b, tn), lambda l, j: (l, 0, j)),
        compiler_params=_params(2),
        name="modulation",
    )(c, mod_w, mod_b.reshape(depth, 1, n))


def _proj_kernel(*refs, n_groups, col_blocks_per_group, rotary, n_casts):
    x_ref, gain_ref, shift_ref, scale_ref = refs[:4]
    rest = list(refs[4:])
    perm_ref = rest.pop(0) if n_groups > 1 else None
    cos_ref, sin_ref = (rest.pop(0), rest.pop(0)) if rotary else (None, None)
    w_ref = rest.pop(0)
    cast_src = [rest.pop(0) for _ in range(n_casts)]
    o_ref = rest.pop(0)
    cast_dst = [rest.pop(0) for _ in range(n_casts)]
    (h_ref,) = rest
    j = pl.program_id(1)
    tm = x_ref.shape[0]

    @pl.when(j == 0)
    def _():
        _norm_mod_rows(x_ref, gain_ref, shift_ref, scale_ref, h_ref.at[0])
        for g in range(1, n_groups):
            for t0 in range(0, tm, PERM_TILE):
                rs = slice(t0, t0 + PERM_TILE)
                h_ref[g, rs, :] = jnp.dot(perm_ref[g - 1], h_ref[0, rs, :],
                                          preferred_element_type=F32).astype(BF16)

    g = j // col_blocks_per_group
    y = jnp.dot(h_ref[g], w_ref[...], preferred_element_type=F32)
    if rotary:
        cos, sin = cos_ref[...], sin_ref[...]
        half = cos.shape[1]
        for c0 in range(0, y.shape[1], 2 * half):
            y1, y2 = y[:, c0:c0 + half], y[:, c0 + half:c0 + 2 * half]
            o_ref[:, c0:c0 + half] = (y1 * cos - y2 * sin).astype(o_ref.dtype)
            o_ref[:, c0 + half:c0 + 2 * half] = (y1 * sin + y2 * cos).astype(o_ref.dtype)
    else:
        o_ref[...] = y.astype(o_ref.dtype)

    for src, dst in zip(cast_src, cast_dst):
        dst[...] = src[...].astype(dst.dtype)


def _cast_row_block(n_rows, n_steps):
    br = CAST_ROW_ALIGN
    while n_rows % br or n_rows // br > n_steps:
        br *= 2
    return br


def _project(x2, seq, gain, shift, scale, w, dilations, rotary=None, casts=()):
    m, d = x2.shape
    n = w.shape[1]
    tm = PROJ_ROW_TILE
    n_groups = len(dilations)

    def vmem_bytes(tn):
        steps = (m // tm) * (n // tn)
        cast = sum(2 * _cast_row_block(a.shape[0], steps) * a.shape[1] * (4 + 2) for a in casts)
        return (2 * (tm * d * 4 + d * tn * 2 + tm * tn * 2) + n_groups * tm * d * 2
                + tm * tn * 4 + cast)

    tn = max(t for t in PROJ_COL_TILES if vmem_bytes(t) <= PROJ_VMEM_BUDGET_BYTES)
    tiles_per_seq = seq // tm
    batch = lambda i, j: (i // tiles_per_seq, 0, 0)
    in_specs = [
        pl.BlockSpec((tm, d), lambda i, j: (i, 0)),
        pl.BlockSpec((1, d), lambda i, j: (0, 0)),
        pl.BlockSpec((None, 1, d), batch),
        pl.BlockSpec((None, 1, d), batch),
    ]
    args = [x2, gain, shift, scale]
    if n_groups > 1:
        assert dilations[0] == 1
        perms = jnp.asarray(np.stack([_class_major_perm(dl) for dl in dilations[1:]]), BF16)
        in_specs.append(pl.BlockSpec(perms.shape, lambda i, j: (0, 0, 0)))
        args.append(perms)
    if rotary is not None:
        cos, sin, cols_per_kind = rotary
        blocks_per_kind = cols_per_kind // tn
        last_kind = cos.shape[0] - 1
        table = lambda i, j: (jnp.minimum(j // blocks_per_kind, last_kind),
                              i % tiles_per_seq, 0)
        in_specs += [pl.BlockSpec((None, tm, cos.shape[2]), table)] * 2
        args += [cos, sin]
    in_specs.append(pl.BlockSpec((d, tn), lambda i, j: (0, j)))
    args.append(w)
    nj = n // tn
    out_specs = [pl.BlockSpec((tm, tn), lambda i, j: (i, j))]
    out_shapes = [jax.ShapeDtypeStruct((m, n), BF16)]
    for a in casts:
        br = _cast_row_block(a.shape[0], (m // tm) * nj)
        last = a.shape[0] // br - 1
        spec = pl.BlockSpec((br, a.shape[1]),
                            lambda i, j, last=last: (jnp.minimum(i * nj + j, last), 0))
        in_specs.append(spec)
        args.append(a)
        out_specs.append(spec)
        out_shapes.append(jax.ShapeDtypeStruct(a.shape, BF16))
    return pl.pallas_call(
        functools.partial(_proj_kernel, n_groups=n_groups,
                          col_blocks_per_group=n // n_groups // tn,
                          rotary=rotary is not None, n_casts=len(casts)),
        out_shape=out_shapes,
        grid=(m // tm, nj),
        in_specs=in_specs,
        out_specs=out_specs,
        scratch_shapes=[pltpu.VMEM((n_groups, tm, d), BF16)],
        compiler_params=_params(2),
        name="in_project",
    )(*args)


def _row_pieces(c, r0, n):
    pieces = []
    r = r0
    while r < r0 + n:
        stop = min((r // c + 1) * c, r0 + n)
        pieces.append((r // c, r % c, stop - (r // c) * c))
        r = stop
    return pieces


def _load_rows(ref, r0, n, cs):
    parts = [ref[k, a:b, cs] for k, a, b in _row_pieces(ref.shape[1], r0, n)]
    return parts[0] if len(parts) == 1 else jnp.concatenate(parts, axis=0)


def _store_rows(ref, r0, cs, val):
    off = 0
    for k, a, b in _row_pieces(ref.shape[1], r0, val.shape[0]):
        ref[k, a:b, cs] = val[off:off + b - a]
        off += b - a


def _attn_kernel(q_ref, k_ref, v_ref, kp_ref, vp_ref, o_ref, lse_ref, *, tq, hb):
    n = pl.program_id(2)
    hblk = pl.program_id(3)
    w = ATTN_STEPS
    nq = tq // w
    scale = HEAD_DIM_A ** -0.5
    all_lanes = slice(0, LANES)

    @pl.when(hblk == 0)
    def _():
        lse_ref[...] = jnp.zeros_like(lse_ref)

    qi = lax.broadcasted_iota(jnp.int32, (w, 2 * w), 0)
    kj = lax.broadcasted_iota(jnp.int32, (w, 2 * w), 1)
    band = (kj >= qi) & (kj <= qi + w)
    band_first = band & ((kj >= w) | (n > 0))
    lane = lax.broadcasted_iota(jnp.int32, (w, LANES), 1)

    for hh in range(hb):
        cs = slice(hh * HEAD_DIM_A, (hh + 1) * HEAD_DIM_A)
        head = hblk * hb + hh
        for qb in range(nq):
            q = _load_rows(q_ref, qb * w, w, cs)
            if qb == 0:
                kcat = jnp.concatenate([_load_rows(kp_ref, 0, w, cs),
                                        _load_rows(k_ref, 0, w, cs)], axis=0)
                vcat = jnp.concatenate([_load_rows(vp_ref, 0, w, cs),
                                        _load_rows(v_ref, 0, w, cs)], axis=0)
                mask = band_first
            else:
                kcat = _load_rows(k_ref, (qb - 1) * w, 2 * w, cs)
                vcat = _load_rows(v_ref, (qb - 1) * w, 2 * w, cs)
                mask = band
            s = lax.dot_general(q, kcat, (((1,), (1,)), ((), ())),
                                preferred_element_type=F32)
            s = jnp.where(mask, s, -jnp.inf)
            m = jnp.max(s, axis=-1, keepdims=True)
            p = jnp.exp2((s - m) * (scale * LOG2_E))
            den = jnp.sum(p, axis=-1, keepdims=True)
            o = jnp.dot(p.astype(BF16), vcat, preferred_element_type=F32) / den
            _store_rows(o_ref, qb * w, cs, o.astype(o_ref.dtype))
            lse = m * scale + jnp.log(den)
            prev = _load_rows(lse_ref, qb * w, w, all_lanes)
            _store_rows(lse_ref, qb * w, all_lanes, jnp.where(lane == head, lse, prev))


def _dilated_attention(qkv, seq, group, dilation):
    m, n_all = qkv.shape
    b = m // seq
    d = dilation
    c = PERM_TILE // d
    tiles = seq // PERM_TILE
    tq = min(ATTN_ROW_TILE, seq // d)
    tpq = tq // c
    hb = ATTN_HEADS_PER_STEP[tq]
    cw = hb * HEAD_DIM_A
    ncb = WIDTH_A // cw
    w = ATTN_STEPS
    col0 = group * 3 * ncb
    pc = min(c, w)
    ptiles = w // pc
    rows_view = lambda a: a.reshape(b, tiles, d, c, a.shape[-1])

    def prev_map(section):
        if c >= w:
            return lambda bi, r, nn, h: (bi, jnp.maximum(nn * tpq - 1, 0), r, c // w - 1,
                                         col0 + section * ncb + h)
        return lambda bi, r, nn, h: (bi, jnp.maximum(nn * (tpq // ptiles) - 1, 0), r, 0,
                                     col0 + section * ncb + h)

    def cur_map(section):
        return lambda bi, r, nn, h: (bi, nn, r, 0, col0 + section * ncb + h)

    qv = rows_view(qkv)
    o, lse = pl.pallas_call(
        functools.partial(_attn_kernel, tq=tq, hb=hb),
        out_shape=(jax.ShapeDtypeStruct((b, tiles, d, c, WIDTH_A), BF16),
                   jax.ShapeDtypeStruct((b, tiles, d, c, LANES), F32)),
        grid=(b, d, seq // d // tq, ncb),
        in_specs=[
            pl.BlockSpec((None, tpq, None, c, cw), cur_map(0)),
            pl.BlockSpec((None, tpq, None, c, cw), cur_map(1)),
            pl.BlockSpec((None, tpq, None, c, cw), cur_map(2)),
            pl.BlockSpec((None, ptiles, None, pc, cw), prev_map(1)),
            pl.BlockSpec((None, ptiles, None, pc, cw), prev_map(2)),
        ],
        out_specs=(
            pl.BlockSpec((None, tpq, None, c, cw), lambda bi, r, nn, h: (bi, nn, r, 0, h)),
            pl.BlockSpec((None, tpq, None, c, LANES), lambda bi, r, nn, h: (bi, nn, r, 0, 0)),
        ),
        compiler_params=_params(4),
        name=f"dilated_attention_d{d}",
    )(qv, qv, qv, qv, qv)
    return o.reshape(m, WIDTH_A), lse.reshape(m, LANES)


def _merge_tile(o_refs, lse_refs, pt_ref, lsn_refs, a_ref, dilations):
    perm_tiles = range(0, a_ref.shape[0], PERM_TILE)
    ls = [lse_refs[0][...]]
    for g in range(1, len(dilations)):
        d = dilations[g]
        c = PERM_TILE // d
        for t0 in perm_tiles:
            for r in range(d):
                lsn_refs[g - 1][pl.ds(t0 + r, c, stride=d), :] = (
                    lse_refs[g][t0 + r * c:t0 + (r + 1) * c, :])
        ls.append(lsn_refs[g - 1][...])
    mx = functools.reduce(jnp.maximum, ls)
    es = [jnp.exp(l - mx) for l in ls]
    inv = 1.0 / functools.reduce(lambda u, v: u + v, es)
    alphas = [e * inv for e in es[1:]]
    heads_per_dot = 2
    for hp in range(N_HEADS_A // heads_per_dot):
        cs2 = slice(hp * heads_per_dot * HEAD_DIM_A, (hp + 1) * heads_per_dot * HEAD_DIM_A)
        base = o_refs[0][:, cs2].astype(F32)
        others = [jnp.concatenate(
            [jnp.dot(pt_ref[g - 1], o_refs[g][t0:t0 + PERM_TILE, cs2],
                     preferred_element_type=F32) for t0 in perm_tiles], axis=0)
            for g in range(1, len(dilations))]
        for hh in range(heads_per_dot):
            h = hp * heads_per_dot + hh
            ls_ = slice(hh * HEAD_DIM_A, (hh + 1) * HEAD_DIM_A)
            acc = base[:, ls_]
            for al, og in zip(alphas, others):
                acc = acc + al[:, h:h + 1] * (og[:, ls_] - base[:, ls_])
            a_ref[:, h * HEAD_DIM_A:(h + 1) * HEAD_DIM_A] = acc.astype(a_ref.dtype)


def _attn_out_kernel(o0, o1, o2, l0, l1, l2, pt_ref, w_ref, x_ref, gate_ref, out_ref,
                     a0_ref, a1_ref, lsn1_ref, lsn2_ref, *, dilations):
    s = pl.program_id(0)

    @pl.when(s == 0)
    def _():
        a1_ref[...] = jnp.zeros_like(a1_ref)

    def step(a_new, a_old):
        _merge_tile((o0, o1, o2), (l0, l1, l2), pt_ref, (lsn1_ref, lsn2_ref), a_new, dilations)
        y = jnp.dot(a_old[...], w_ref[...], preferred_element_type=F32)
        out_ref[...] = x_ref[...] + gate_ref[...] * y

    @pl.when(s % 2 == 0)
    def _():
        step(a0_ref, a1_ref)

    @pl.when(s % 2 == 1)
    def _():
        step(a1_ref, a0_ref)


def _attn_out_project(x2, seq, gate, w, outs, lses, dilations):
    m, d = x2.shape
    k = w.shape[0]
    tm = ATTN_OUT_ROW_TILE
    n_tiles = m // tm
    tiles_per_seq = seq // tm
    pts = jnp.asarray(np.stack([_class_major_perm(dl).T for dl in dilations[1:]]), BF16)
    cur = lambda s: (jnp.minimum(s, n_tiles - 1), 0)
    prev = lambda s: (jnp.maximum(s - 1, 0), 0)
    in_specs = ([pl.BlockSpec((tm, k), cur) for _ in outs]
                + [pl.BlockSpec((tm, LANES), cur) for _ in lses]
                + [pl.BlockSpec(pts.shape, lambda s: (0, 0, 0)),
                   pl.BlockSpec((k, d), lambda s: (0, 0), pipeline_mode=pl.Buffered(1)),
                   pl.BlockSpec((tm, d), prev),
                   pl.BlockSpec((None, 1, d),
                                lambda s: (jnp.maximum(s - 1, 0) // tiles_per_seq, 0, 0))])
    return pl.pallas_call(
        functools.partial(_attn_out_kernel, dilations=dilations),
        out_shape=jax.ShapeDtypeStruct((m, d), F32),
        grid=(n_tiles + 1,),
        in_specs=in_specs,
        out_specs=pl.BlockSpec((tm, d), prev),
        scratch_shapes=[pltpu.VMEM((tm, k), BF16), pltpu.VMEM((tm, k), BF16),
                        pltpu.VMEM((tm, LANES), F32), pltpu.VMEM((tm, LANES), F32)],
        compiler_params=_params(1),
        name="attn_out_project",
    )(*outs, *lses, pts, w, x2, gate)


def _out_kernel(a_ref, w_ref, x_ref, gate_ref, out_ref):
    y = jnp.dot(a_ref[...], w_ref[...], preferred_element_type=F32)
    out_ref[...] = x_ref[...] + gate_ref[...] * y


def _out_project(x2, seq, gate, w, a):
    m, d = x2.shape
    k = w.shape[0]
    tm, tn = OUT_ROW_TILE, OUT_COL_TILE
    tiles_per_seq = seq // tm
    return pl.pallas_call(
        _out_kernel,
        out_shape=jax.ShapeDtypeStruct((m, d), F32),
        grid=(m // tm, d // tn),
        in_specs=[
            pl.BlockSpec((tm, k), lambda i, j: (i, 0)),
            pl.BlockSpec((k, tn), lambda i, j: (0, j)),
            pl.BlockSpec((tm, tn), lambda i, j: (i, j)),
            pl.BlockSpec((None, 1, tn), lambda i, j: (i // tiles_per_seq, 0, j)),
        ],
        out_specs=pl.BlockSpec((tm, tn), lambda i, j: (i, j)),
        compiler_params=_params(2),
        name="out_project",
    )(a, w, x2, gate)


def _retention_kernel(q_ref, k_ref, v_ref, g_ref, decay_ref, xi_ref, zeta_ref, gamc_ref,
                      gn_ref, o_ref, r_ref, *, tc):
    c = RET_CHUNK

    @pl.when(pl.program_id(2) == 0)
    def _():
        r_ref[...] = jnp.zeros_like(r_ref)

    decay = decay_ref[...]
    xi = xi_ref[...]
    zeta = zeta_ref[...]
    gamc = gamc_ref[...]
    gn = gn_ref[...]

    for ci in range(tc // c):
        rs = slice(ci * c, (ci + 1) * c)
        q = q_ref[rs, :]
        k = k_ref[rs, :]
        v = v_ref[rs, :]
        s = lax.dot_general(q, k, (((1,), (1,)), ((), ())), preferred_element_type=F32) * decay
        inner = jnp.dot(s.astype(BF16), v, preferred_element_type=F32)
        r = r_ref[...]
        cross = jnp.dot(q, r.astype(BF16), preferred_element_type=F32) * xi
        kz = (k.astype(F32) * zeta).astype(BF16)
        kv = lax.dot_general(kz, v, (((0,), (0,)), ((), ())), preferred_element_type=F32)
        r_ref[...] = gamc * r + kv
        y = inner + cross
        mu = jnp.mean(y, axis=-1, keepdims=True)
        yc = y - mu
        var = jnp.mean(yc * yc, axis=-1, keepdims=True)
        yn = yc * lax.rsqrt(var + EPS) * gn
        g = g_ref[rs, :].astype(F32)
        o_ref[rs, :] = (_silu(g) * yn).astype(o_ref.dtype)


def _rotary_tables(seq, dk):
    half = dk // 2
    pos = jnp.arange(seq, dtype=F32)
    freqs = ROPE_BASE ** (-jnp.arange(half, dtype=F32) / half)
    ang = pos[:, None] * freqs[None, :]
    cos, sin = jnp.cos(ang), jnp.sin(ang)
    kscale = dk ** -0.5
    return (jnp.stack([cos, cos * kscale, jnp.ones_like(cos)]),
            jnp.stack([sin, sin * kscale, jnp.zeros_like(sin)]))


def _retention_tables():
    h, c = N_HEADS_B, RET_CHUNK
    log_gamma = jnp.log1p(-jnp.exp2(-5.0 - jnp.arange(h, dtype=F32)))
    idx = jnp.arange(c, dtype=F32)
    rel = idx[:, None] - idx[None, :]
    decay = jnp.where(rel >= 0, jnp.exp(log_gamma[:, None, None] * jnp.maximum(rel, 0.0)), 0.0)
    xi = jnp.exp(log_gamma[:, None] * (idx + 1.0))[:, :, None]
    zeta = jnp.exp(log_gamma[:, None] * (c - 1.0 - idx))[:, :, None]
    gamc = jnp.exp(log_gamma * c)[:, None, None]
    return decay, xi, zeta, gamc


def _retention(proj, gn_g):
    b, s, n = proj.shape
    h = N_HEADS_B
    dv = gn_g.shape[-1] // h
    dk = (n - 2 * h * dv) // (2 * h)
    tc = RET_ROW_TILE
    c = RET_CHUNK
    decay, xi, zeta, gamc = _retention_tables()
    v_blk0 = (2 * h * dk) // dv
    out = pl.pallas_call(
        functools.partial(_retention_kernel, tc=tc),
        out_shape=jax.ShapeDtypeStruct((b, s, h * dv), BF16),
        grid=(b, h, s // tc),
        in_specs=[
            pl.BlockSpec((None, tc, dk), lambda bi, hi, t: (bi, t, hi)),
            pl.BlockSpec((None, tc, dk), lambda bi, hi, t: (bi, t, h + hi)),
            pl.BlockSpec((None, tc, dv), lambda bi, hi, t: (bi, t, v_blk0 + hi)),
            pl.BlockSpec((None, tc, dv), lambda bi, hi, t: (bi, t, v_blk0 + h + hi)),
            pl.BlockSpec((None, c, c), lambda bi, hi, t: (hi, 0, 0)),
            pl.BlockSpec((None, c, 1), lambda bi, hi, t: (hi, 0, 0)),
            pl.BlockSpec((None, c, 1), lambda bi, hi, t: (hi, 0, 0)),
            pl.BlockSpec((None, 1, 1), lambda bi, hi, t: (hi, 0, 0)),
            pl.BlockSpec((1, dv), lambda bi, hi, t: (0, hi)),
        ],
        out_specs=pl.BlockSpec((None, tc, dv), lambda bi, hi, t: (bi, t, hi)),
        scratch_shapes=[pltpu.VMEM((dk, dv), F32)],
        compiler_params=_params(3),
        name="retention",
    )(proj, proj, proj, proj, decay, xi, zeta, gamc, gn_g.reshape(1, h * dv))
    return out.reshape(b * s, h * dv)


def _ffn_kernel(x_ref, gain_ref, shift_ref, scale_ref, gate_ref, wa_ref, wb_ref,
                conv_ref, wd_ref, fg_ref, o_ref, h_ref,
                ua_ref, ub_ref, carry_ref, *, tiles_per_seq, final_norm):
    i = pl.program_id(0)
    f = pl.program_id(1)
    pad = CONV_HALO_ROWS
    tm = x_ref.shape[0]
    u_refs = (ua_ref, ub_ref)

    @pl.when(f == 0)
    def _():
        _norm_mod_rows(x_ref, gain_ref, shift_ref, scale_ref, h_ref)
        o_ref[...] = jnp.zeros_like(o_ref)

    @pl.when(i % tiles_per_seq == 0)
    def _():
        for u_ref in u_refs:
            u_ref[0:pad, :] = jnp.zeros((pad, u_ref.shape[1]), F32)

    @pl.when(i % tiles_per_seq != 0)
    def _():
        for br, u_ref in enumerate(u_refs):
            u_ref[0:pad, :] = carry_ref[f, br]

    def up_project(r0, r1):
        hr = h_ref[r0:r1, :]
        for u_ref, w_ref in zip(u_refs, (wa_ref, wb_ref)):
            u_ref[pad + r0:pad + r1, :] = jnp.dot(hr, w_ref[...], preferred_element_type=F32)

    nf = pl.num_programs(1)
    conv_params = (conv_ref[f], conv_ref[nf + f])

    def conv(u_ref, cp, r0, r1):
        y = cp[2:3, :] * u_ref[pad + r0:pad + r1, :]
        y = y + cp[1:2, :] * u_ref[pad - 1 + r0:pad - 1 + r1, :]
        y = y + cp[0:1, :] * u_ref[pad - 2 + r0:pad - 2 + r1, :]
        return y + cp[CONV_WIDTH:CONV_WIDTH + 1, :]

    def down_project(r0, r1):
        a = conv(ua_ref, conv_params[0], r0, r1)
        b = conv(ub_ref, conv_params[1], r0, r1)
        act = (_silu(a) * b).astype(BF16)
        o_ref[r0:r1, :] += jnp.dot(act, wd_ref[...], preferred_element_type=F32)

    bounds = list(range(0, tm + 1, FFN_ROW_CHUNK))
    chunks = list(zip(bounds[:-1], bounds[1:]))
    up_project(*chunks[0])
    for ci, (r0, r1) in enumerate(chunks):
        if ci + 1 < len(chunks):
            up_project(*chunks[ci + 1])
        down_project(r0, r1)

    for br, u_ref in enumerate(u_refs):
        carry_ref[f, br] = u_ref[tm:tm + pad, :]

    @pl.when(f == pl.num_programs(1) - 1)
    def _():
        xn = x_ref[...] + gate_ref[...] * o_ref[...]
        if final_norm:
            xn = _rmsnorm(xn, fg_ref[...])
        o_ref[...] = xn


def _pack_conv_params(conv_w, conv_b, tf):
    depth = conv_w.shape[0]
    blocks = conv_w.shape[-1] // tf
    rows = jnp.concatenate([conv_w.reshape(depth, CONV_WIDTH, blocks, tf),
                            conv_b.reshape(depth, 1, blocks, tf)], axis=1)
    rows = jnp.pad(rows, ((0, 0), (0, CONV_HALO_ROWS - CONV_WIDTH - 1), (0, 0), (0, 0)))
    return rows.transpose(0, 2, 1, 3)


def _conv_ffn(x2, seq, gain, shift, scale, gate, w_up, conv_pack, w_down, final_gain,
              layer, final_norm):
    m, d = x2.shape
    ff = w_down.shape[1]
    tm, tf = FFN_ROW_TILE, FFN_COL_TILE
    nf = ff // tf
    tiles_per_seq = seq // tm
    batch = lambda i, f: (i // tiles_per_seq, 0, 0)
    const = lambda i, f: (0, 0)
    return pl.pallas_call(
        functools.partial(_ffn_kernel, tiles_per_seq=tiles_per_seq, final_norm=final_norm),
        out_shape=jax.ShapeDtypeStruct((m, d), F32),
        grid=(m // tm, nf),
        in_specs=[
            pl.BlockSpec((tm, d), lambda i, f: (i, 0), pipeline_mode=pl.Buffered(1)),
            pl.BlockSpec((1, d), const),
            pl.BlockSpec((None, 1, d), batch),
            pl.BlockSpec((None, 1, d), batch),
            pl.BlockSpec((None, 1, d), batch),
            pl.BlockSpec((None, d, tf), lambda i, f: (layer, 0, f)),
            pl.BlockSpec((None, d, tf), lambda i, f: (layer, 0, nf + f)),
            pl.BlockSpec((None,) + conv_pack.shape[1:], lambda i, f: (layer, 0, 0, 0)),
            pl.BlockSpec((None, tf, d), lambda i, f: (layer, f, 0)),
            pl.BlockSpec((1, d), const),
        ],
        out_specs=pl.BlockSpec((tm, d), lambda i, f: (i, 0)),
        scratch_shapes=[pltpu.VMEM((tm, d), BF16),
                        pltpu.VMEM((tm + CONV_HALO_ROWS, tf), F32),
                        pltpu.VMEM((tm + CONV_HALO_ROWS, tf), F32),
                        pltpu.VMEM((nf, 2, CONV_HALO_ROWS, tf), F32)],
        compiler_params=_params(2),
        name="conv_ffn",
    )(x2, gain, shift, scale, gate, w_up, w_up, conv_pack, w_down, final_gain)


def kernel(x, c, mod_w, mod_b, norm_mix, w_in_a, w_out_a, w_in_b, gn_b, w_out_b, norm_ffn,
           ffn_up, ffn_conv_w, ffn_conv_b, ffn_down, final_norm):
    b, s, d = x.shape
    depth = mod_w.shape[0]
    assert depth == 2, "layer 0 = dilated attention mixer, layer 1 = retention mixer"
    mod = _modulation(c, mod_w, mod_b).reshape(depth, b, 6, 1, d)
    x2 = x.reshape(b * s, d)
    dilations = tuple(dl for _, dl in DILATED_GROUPS)
    conv_pack = _pack_conv_params(ffn_conv_w, ffn_conv_b, FFN_COL_TILE)
    later = (w_out_a, w_in_b, w_out_b, ffn_up, ffn_down)
    later_2d = tuple(a.reshape(-1, a.shape[-1]) for a in later)

    for i in range(depth):
        sh_a, sc_a, g_a, sh_m, sc_m, g_m = (mod[i, :, k] for k in range(6))
        gain = norm_mix[i].reshape(1, d)
        j = i // 2
        if i == 0:
            qkv, *cast = _project(x2, s, gain, sh_a, sc_a, w_in_a[j].astype(BF16), dilations,
                                  casts=later_2d)
            w_out_a_bf, w_in_b_bf, w_out_b_bf, ffn_up_bf, ffn_down_bf = (
                cb.reshape(a.shape) for cb, a in zip(cast, later))
            outs, lses = [], []
            for g, dil in enumerate(dilations):
                o, lse = _dilated_attention(qkv, s, g, dil)
                outs.append(o)
                lses.append(lse)
            x2 = _attn_out_project(x2, s, g_a, w_out_a_bf[j], outs, lses, dilations)
        else:
            dv = gn_b.shape[-1] // N_HEADS_B
            qk_cols = (w_in_b.shape[-1] - 2 * N_HEADS_B * dv) // 2
            cos, sin = _rotary_tables(s, qk_cols // N_HEADS_B)
            (proj,) = _project(x2, s, gain, sh_a, sc_a, w_in_b_bf[j], (1,),
                               rotary=(cos, sin, qk_cols))
            a = _retention(proj.reshape(b, s, -1), gn_b[j])
            x2 = _out_project(x2, s, g_a, w_out_b_bf[j], a)
        x2 = _conv_ffn(
            x2, s, norm_ffn[i].reshape(1, d), sh_m, sc_m, g_m, ffn_up_bf, conv_pack,
            ffn_down_bf, final_norm.reshape(1, d), layer=i, final_norm=(i == depth - 1))
    return x2.reshape(b, s, d)
```
